```python
import math
import jax
import jax.numpy as jnp
from jax import lax
import numpy as np

D_MODEL = 1024
BATCH = 8
SEQ = 4096
DEPTH = 4

CHUNK = 64
HEAD_DIM = 64
EPS = 1e-6

ATTN_HEADS = 8
ATTN_WIDTH = ATTN_HEADS * HEAD_DIM
LEFT_CHUNKS = 8
BAND_CHUNKS = LEFT_CHUNKS + 1
BAND_LEN = BAND_CHUNKS * CHUNK
MAX_REL = 128
REL_TABLE = MAX_REL + CHUNK

SSM_WIDTH = D_MODEL // 4
SSM_GROUP = 16
SSM_GROUPS = SSM_WIDTH // SSM_GROUP
SSM_STATE = 64
DT_MIN = 1e-3
DT_MAX = 1e-1

RET_HEADS = 4
RET_WIDTH = RET_HEADS * HEAD_DIM
ROPE_BASE = 10000.0

MIX_WIDTH = ATTN_WIDTH + SSM_WIDTH + RET_WIDTH
IN_WIDTH = 3 * ATTN_WIDTH + SSM_WIDTH + 4 * RET_WIDTH

N_GROUPS = 4
EXPERTS_PER_GROUP = 8
N_EXPERTS = N_GROUPS * EXPERTS_PER_GROUP
TOP_K = 2
EXPERT_FF = 256

kernel_name = "hybrid_streaming_encoder_hmoe"


def _rmsnorm(x, g):
    xf = x.astype(jnp.float32)
    y = xf * lax.rsqrt(jnp.mean(xf * xf, axis=-1, keepdims=True) + EPS)
    return (y * g.astype(jnp.float32)).astype(x.dtype)


def _rotary(x, pos):
    half = x.shape[-1] // 2
    inv_freq = ROPE_BASE ** (-jnp.arange(half, dtype=jnp.float32) / half)
    ang = pos.astype(jnp.float32)[:, None] * inv_freq[None, :]
    cos = jnp.cos(ang)[None, :, None, :]
    sin = jnp.sin(ang)[None, :, None, :]
    x1, x2 = x[..., :half], x[..., half:]
    return jnp.concatenate([x1 * cos - x2 * sin, x1 * sin + x2 * cos], axis=-1)


def _chunk_band_attention(q, k, v, rel_bias):
    b, s, h, dh = q.shape
    nc = s // CHUNK
    f32 = jnp.float32
    qc = q.reshape(b, nc, CHUNK, h, dh)
    pad = ((0, 0), (LEFT_CHUNKS, 0), (0, 0), (0, 0), (0, 0))
    kp = jnp.pad(k.reshape(b, nc, CHUNK, h, dh), pad)
    vp = jnp.pad(v.reshape(b, nc, CHUNK, h, dh), pad)
    band = jnp.arange(nc)[:, None] + jnp.arange(BAND_CHUNKS)[None, :]
    kb = kp[:, band].reshape(b, nc, BAND_LEN, h, dh)
    vb = vp[:, band].reshape(b, nc, BAND_LEN, h, dh)
    valid = jnp.repeat(band >= LEFT_CHUNKS, CHUNK, axis=1)
    key_off = jnp.arange(BAND_LEN) - LEFT_CHUNKS * CHUNK
    rel = key_off[None, :] - jnp.arange(CHUNK)[:, None]
    rel_idx = jnp.clip(rel, -MAX_REL, CHUNK - 1) + MAX_REL
    bias = rel_bias.astype(f32)[:, rel_idx]
    scores = jnp.einsum('bnqhd,bnkhd->bnhqk', qc, kb,
                        preferred_element_type=f32) * (dh ** -0.5) + bias
    scores = jnp.where(valid[None, :, None, None, :], scores, -jnp.inf)
    probs = jax.nn.softmax(scores, axis=-1).astype(v.dtype)
    out = jnp.einsum('bnhqk,bnkhd->bnqhd', probs, vb)
    return out.reshape(b, s, h * dh)


def _linear_recurrence_combine(left, right):
    a_l, b_l = left
    a_r, b_r = right
    return a_r * a_l, a_r * b_l + b_r


def _s5(u, a_re, a_im, log_dt, b_re, b_im, c_re, c_im, d_skip, w_glu, b_glu):
    bsz, s, _ = u.shape
    f32 = jnp.float32
    lam = lax.complex(a_re.astype(f32), a_im.astype(f32))
    dt = jnp.exp(log_dt.astype(f32))[:, None]
    a_bar = jnp.exp(lam * dt)
    b_mat = lax.complex(b_re.astype(f32), b_im.astype(f32))
    b_bar = ((a_bar - 1.0) / lam)[:, :, None] * b_mat
    ug = u.astype(f32).reshape(bsz, s, SSM_GROUPS, SSM_GROUP)
    bu = jnp.einsum('bsgc,gpc->bsgp', ug.astype(jnp.complex64), b_bar)
    a_seq = jnp.broadcast_to(a_bar, bu.shape)
    _, states = lax.associative_scan(_linear_recurrence_combine, (a_seq, bu), axis=1)
    c_mat = lax.complex(c_re.astype(f32), c_im.astype(f32))
    y = jnp.real(jnp.einsum('bsgp,gcp->bsgc', states, c_mat))
    y = y + d_skip.astype(f32).reshape(SSM_GROUPS, SSM_GROUP) * ug
    y = jax.nn.gelu(y.reshape(bsz, s, SSM_WIDTH))
    y = y * jax.nn.sigmoid(y @ w_glu.astype(f32) + b_glu.astype(f32))
    return y.astype(u.dtype)


def _retention(q, k, v, gate, gn_g, pos):
    b, s, h, dh = q.shape
    nc = s // CHUNK
    f32 = jnp.float32
    q = _rotary(q.astype(f32), pos)
    k = _rotary(k.astype(f32), pos) * (dh ** -0.5)
    v = v.astype(f32)
    log_gamma = jnp.log1p(-jnp.exp2(-5.0 - jnp.arange(h, dtype=f32)))
    t = jnp.arange(CHUNK, dtype=f32)
    diff = t[:, None] - t[None, :]
    decay = jnp.where(diff >= 0,
                      jnp.exp(log_gamma[:, None, None] * jnp.maximum(diff, 0.0)), 0.0)
    qc = q.reshape(b, nc, CHUNK, h, dh)
    kc = k.reshape(b, nc, CHUNK, h, dh)
    vc = v.reshape(b, nc, CHUNK, h, dh)
    inner = jnp.einsum('bnihd,bnjhd->bnhij', qc, kc) * decay
    inner = jnp.einsum('bnhij,bnjhe->bnihe', inner, vc)
    zeta = jnp.exp(log_gamma[:, None] * (CHUNK - 1 - t))
    kv = jnp.einsum('bnjhd,hj,bnjhe->nbhde', kc, zeta, vc)
    chunk_decay = jnp.exp(log_gamma * CHUNK)[None, :, None, None]

    def step(state, kv_n):
        return chunk_decay * state + kv_n, state

    _, prev = lax.scan(step, jnp.zeros((b, h, dh, dh), f32), kv)
    xi = jnp.exp(log_gamma[:, None] * (t + 1.0))
    cross = jnp.einsum('bnihd,nbhde,hi->bnihe', qc, prev, xi)
    y = inner + cross
    mu = jnp.mean(y, axis=-1, keepdims=True)
    var = jnp.mean(jnp.square(y - mu), axis=-1, keepdims=True)
    y = ((y - mu) * lax.rsqrt(var + EPS)).reshape(b, s, h * dh) * gn_g.astype(f32)
    return (jax.nn.silu(gate.astype(f32)) * y).astype(gate.dtype)


def _hier_moe(h, w_group, b_group, w_expert, b_expert, w_gate, w_up, w_down):
    bsz, s, d = h.shape
    f32 = jnp.float32
    t = h.reshape(bsz * s, d)
    group_logits = (t @ w_group + b_group).astype(f32)
    group_idx = jnp.argmax(group_logits, axis=-1)
    group_w = jnp.max(jax.nn.softmax(group_logits, axis=-1), axis=-1, keepdims=True)
    expert_logits = (t @ w_expert + b_expert).astype(f32)
    expert_logits = expert_logits.reshape(-1, N_GROUPS, EXPERTS_PER_GROUP)
    in_group = jnp.einsum('ng,nge->ne', jax.nn.one_hot(group_idx, N_GROUPS, dtype=f32), expert_logits)
    top_val, top_idx = lax.top_k(in_group, TOP_K)
    top_w = jax.nn.softmax(top_val, axis=-1) * group_w
    expert_id = group_idx[:, None] * EXPERTS_PER_GROUP + top_idx
    combine = jnp.sum(jax.nn.one_hot(expert_id, N_EXPERTS, dtype=f32) * top_w[..., None], axis=1)
    out = jnp.zeros((t.shape[0], d), f32)
    for gi in range(N_GROUPS):
        e = slice(gi * EXPERTS_PER_GROUP, (gi + 1) * EXPERTS_PER_GROUP)
        hg = jnp.einsum('nd,edf->nef', t, w_gate[e])
        hu = jnp.einsum('nd,edf->nef', t, w_up[e])
        act = jax.nn.silu(hg) * hu * combine[:, e, None].astype(t.dtype)
        out = out + jnp.einsum('nef,efd->nd', act, w_down[e]).astype(f32)
    return out.astype(h.dtype).reshape(bsz, s, d)


def setup_inputs(seed: int = 0) -> dict:
    key = jax.random.key(seed)
    ks = iter(jax.random.split(key, 32))
    f32 = jnp.float32

    def nrm(shape, scale):
        return jax.random.normal(next(ks), shape, f32) * scale

    L, D = DEPTH, D_MODEL
    G, P, Cg = SSM_GROUPS, SSM_STATE, SSM_GROUP
    x = nrm((BATCH, SEQ, D), 1.0)
    c = nrm((BATCH, D), 1.0)
    norm1_g = 1.0 + nrm((L, D), 0.05)
    norm2_g = 1.0 + nrm((L, D), 0.05)
    w_ada = nrm((L, D, 6 * D), 0.5 * D ** -0.5)
    b_ada = nrm((L, 6 * D), 0.01)
    w_in = nrm((L, D, IN_WIDTH), D ** -0.5)
    attn_rel_bias = nrm((L, ATTN_HEADS, REL_TABLE), 0.2)
    ssm_a_re = -0.5 + nrm((L, G, P), 0.01)
    ssm_a_im = jnp.tile(math.pi * jnp.arange(P, dtype=f32), (L, G, 1))
    ssm_log_dt = jax.random.uniform(next(ks), (L, G), f32, math.log(DT_MIN), math.log(DT_MAX))
    ssm_b_re = nrm((L, G, P, Cg), (2 * Cg) ** -0.5)
    ssm_b_im = nrm((L, G, P, Cg), (2 * Cg) ** -0.5)
    ssm_c_re = nrm((L, G, Cg, P), (2 * P) ** -0.5)
    ssm_c_im = nrm((L, G, Cg, P), (2 * P) ** -0.5)
    ssm_d = nrm((L, SSM_WIDTH), 1.0)
    ssm_w_glu = nrm((L, SSM_WIDTH, SSM_WIDTH), SSM_WIDTH ** -0.5)
    ssm_b_glu = nrm((L, SSM_WIDTH), 0.01)
    ret_gn_g = 1.0 + nrm((L, RET_WIDTH), 0.05)
    w_out = nrm((L, MIX_WIDTH, D), MIX_WIDTH ** -0.5)
    moe_w_group = nrm((L, D, N_GROUPS), D ** -0.5)
    moe_b_group = nrm((L, N_GROUPS), 0.01)
    moe_w_expert = nrm((L, D, N_EXPERTS), D ** -0.5)
    moe_b_expert = nrm((L, N_EXPERTS), 0.01)
    moe_w_gate = nrm((L, N_EXPERTS, D, EXPERT_FF), D ** -0.5)
    moe_w_up = nrm((L, N_EXPERTS, D, EXPERT_FF), D ** -0.5)
    moe_w_down = nrm((L, N_EXPERTS, EXPERT_FF, D), EXPERT_FF ** -0.5)
    final_g = 1.0 + nrm((D,), 0.05)
    return {"x": x, "c": c, "norm1_g": norm1_g, "norm2_g": norm2_g,
            "w_ada": w_ada, "b_ada": b_ada, "w_in": w_in, "attn_rel_bias": attn_rel_bias,
            "ssm_a_re": ssm_a_re, "ssm_a_im": ssm_a_im, "ssm_log_dt": ssm_log_dt,
            "ssm_b_re": ssm_b_re, "ssm_b_im": ssm_b_im, "ssm_c_re": ssm_c_re, "ssm_c_im": ssm_c_im,
            "ssm_d": ssm_d, "ssm_w_glu": ssm_w_glu, "ssm_b_glu": ssm_b_glu,
            "ret_gn_g": ret_gn_g, "w_out": w_out,
            "moe_w_group": moe_w_group, "moe_b_group": moe_b_group,
            "moe_w_expert": moe_w_expert, "moe_b_expert": moe_b_expert,
            "moe_w_gate": moe_w_gate, "moe_w_up": moe_w_up, "moe_w_down": moe_w_down,
            "final_g": final_g}


def reference(x, c, norm1_g, norm2_g, w_ada, b_ada, w_in, attn_rel_bias,
              ssm_a_re, ssm_a_im, ssm_log_dt, ssm_b_re, ssm_b_im, ssm_c_re, ssm_c_im,
              ssm_d, ssm_w_glu, ssm_b_glu, ret_gn_g, w_out,
              moe_w_group, moe_b_group, moe_w_expert, moe_b_expert,
              moe_w_gate, moe_w_up, moe_w_down, final_g):
    b, s, _ = x.shape
    pos = jnp.arange(s)
    cond = jax.nn.silu(c)
    split_at = np.cumsum([ATTN_WIDTH] * 3 + [SSM_WIDTH] + [RET_WIDTH] * 3).tolist()
    for i in range(DEPTH):
        mod = (cond @ w_ada[i] + b_ada[i])[:, None, :]
        sh1, sc1, g1, sh2, sc2, g2 = jnp.split(mod, 6, axis=-1)
        h = _rmsnorm(x, norm1_g[i]) * (1.0 + sc1) + sh1
        q_a, k_a, v_a, u_s, q_r, k_r, v_r, g_r = jnp.split(h @ w_in[i], split_at, axis=-1)
        y_a = _chunk_band_attention(q_a.reshape(b, s, ATTN_HEADS, HEAD_DIM),
                                    k_a.reshape(b, s, ATTN_HEADS, HEAD_DIM),
                                    v_a.reshape(b, s, ATTN_HEADS, HEAD_DIM),
                                    attn_rel_bias[i])
        y_s = _s5(u_s, ssm_a_re[i], ssm_a_im[i], ssm_log_dt[i], ssm_b_re[i], ssm_b_im[i],
                  ssm_c_re[i], ssm_c_im[i], ssm_d[i], ssm_w_glu[i], ssm_b_glu[i])
        y_r = _retention(q_r.reshape(b, s, RET_HEADS, HEAD_DIM),
                         k_r.reshape(b, s, RET_HEADS, HEAD_DIM),
                         v_r.reshape(b, s, RET_HEADS, HEAD_DIM),
                         g_r, ret_gn_g[i], pos)
        mixed = jnp.concatenate([y_a.astype(x.dtype), y_s, y_r], axis=-1) @ w_out[i]
        x = x + g1 * mixed
        h = _rmsnorm(x, norm2_g[i]) * (1.0 + sc2) + sh2
        x = x + g2 * _hier_moe(h, moe_w_group[i], moe_b_group[i], moe_w_expert[i], moe_b_expert[i],
                               moe_w_gate[i], moe_w_up[i], moe_w_down[i])
    return _rmsnorm(x, final_g)
```

```python
import functools
import math

import jax
import jax.numpy as jnp
import numpy as np
from jax import lax
from jax.experimental import pallas as pl
from jax.experimental.pallas import tpu as pltpu

F32 = jnp.float32
BF16 = jnp.bfloat16

D_MODEL = 1024
N_LAYERS = 4
CHUNK = 64
HEAD_DIM = 64
NORM_EPS = 1e-6

ATTN_HEADS = 8
ATTN_WIDTH = ATTN_HEADS * HEAD_DIM
LEFT_CHUNKS = 8
MAX_REL = 128
REL_TABLE = MAX_REL + CHUNK

SSM_WIDTH = 256
SSM_GROUP = 16
SSM_GROUPS = 16
SSM_STATE = 64
SSM_COMPLEX = SSM_GROUPS * SSM_STATE

RET_HEADS = 4
RET_WIDTH = RET_HEADS * HEAD_DIM
ROPE_BASE = 10000.0

N_GROUPS = 4
EXPERTS_PER_GROUP = 8
N_EXPERTS = N_GROUPS * EXPERTS_PER_GROUP
EXPERT_FF = 256

SUBLANES = 8
LANES = 128
VMEM_LIMIT = 56 * 1024 * 1024

ATTN_TQ = 256
RET_T = 256
SSM_TS = 64
MOE_TM = 256
ROW_TILES = D_MODEL // LANES
NEG_BIG = -1e30


def _sigmoid(x):
    return 1.0 / (1.0 + jnp.exp(-x))


def _cparams(*sem):
    return pltpu.CompilerParams(dimension_semantics=sem, vmem_limit_bytes=VMEM_LIMIT)


def _adaln_body(c_ref, w_ref, b_ref, o_ref):
    c = c_ref[...]
    cond = c * _sigmoid(c)
    o_ref[...] = jnp.dot(cond, w_ref[...], preferred_element_type=F32,
                         precision=lax.Precision.HIGHEST) + b_ref[...]


def _adaln(c, w_ada, b_ada):
    n_l, d, d6 = w_ada.shape
    b = c.shape[0]
    tn = 1536
    return pl.pallas_call(
        _adaln_body,
        grid=(n_l, d6 // tn),
        in_specs=[pl.BlockSpec((b, d), lambda l, j: (0, 0)),
                  pl.BlockSpec((None, d, tn), lambda l, j: (l, 0, j)),
                  pl.BlockSpec((None, 1, tn), lambda l, j: (l, 0, j))],
        out_specs=pl.BlockSpec((None, b, tn), lambda l, j: (l, 0, j)),
        out_shape=jax.ShapeDtypeStruct((n_l, b, d6), F32),
        compiler_params=_cparams("arbitrary", "arbitrary"),
        name="adaln",
    )(c, w_ada, b_ada.reshape(n_l, 1, d6))


def _modulated_norm(x, g, sc, sh):
    ms = jnp.mean(x * x, axis=-1, keepdims=True)
    return (x * lax.rsqrt(ms + NORM_EPS) * g) * (1.0 + sc) + sh


def _inproj_body(x_ref, sc_ref, sh_ref, g_ref, w_ref, cos_ref, sin_ref,
                 qa_ref, ka_ref, va_ref, us_ref, qr_ref, kr_ref, vr_ref, gr_ref):
    hb = _modulated_norm(x_ref[...], g_ref[...], sc_ref[...], sh_ref[...]).astype(BF16)

    def proj(lo, width):
        return jnp.dot(hb, w_ref[:, lo:lo + width], preferred_element_type=F32)

    scale = HEAD_DIM ** -0.5
    a = ATTN_WIDTH
    qa_ref[...] = (proj(0, a) * scale).astype(BF16)
    ka_ref[...] = proj(a, a).astype(BF16)
    va_ref[...] = proj(2 * a, a).astype(BF16)
    o = 3 * a
    us_ref[...] = proj(o, SSM_WIDTH)
    o += SSM_WIDTH

    cos = cos_ref[...]
    sin = sin_ref[...]
    lane = lax.broadcasted_iota(jnp.int32, cos.shape, 1)
    first_half = (lane & (HEAD_DIM // 2)) == 0

    def rotary(z):
        partner = jnp.where(first_half,
                            pltpu.roll(z, RET_WIDTH - HEAD_DIM // 2, 1),
                            pltpu.roll(z, HEAD_DIM // 2, 1))
        return z * cos + partner * sin

    r = RET_WIDTH
    qr_ref[...] = rotary(proj(o, r)).astype(BF16)
    kr_ref[...] = (rotary(proj(o + r, r)) * scale).astype(BF16)
    vr_ref[...] = proj(o + 2 * r, r).astype(BF16)
    gr_ref[...] = proj(o + 3 * r, r)


def _inproj(x, sc, sh, g, w_bf16, cos_t, sin_t, ts):
    b, s, d = x.shape
    nj = s // ts
    tok = lambda width: pl.BlockSpec((None, ts, width), lambda bi, j: (bi, j, 0))
    per_b = pl.BlockSpec((None, 1, d), lambda bi, j: (bi, 0, 0))
    rope = pl.BlockSpec((ts, RET_WIDTH), lambda bi, j: (j, 0))
    sds = lambda width, dt: jax.ShapeDtypeStruct((b, s, width), dt)
    return pl.pallas_call(
        _inproj_body,
        grid=(b, nj),
        in_specs=[tok(d), per_b, per_b,
                  pl.BlockSpec((1, d), lambda bi, j: (0, 0)),
                  pl.BlockSpec(w_bf16.shape, lambda bi, j: (0, 0)),
                  rope, rope],
        out_specs=[tok(ATTN_WIDTH), tok(ATTN_WIDTH), tok(ATTN_WIDTH),
                   pl.BlockSpec((ts, SSM_WIDTH), lambda bi, j: (j, bi)),
                   tok(RET_WIDTH), tok(RET_WIDTH), tok(RET_WIDTH), tok(RET_WIDTH)],
        out_shape=[sds(ATTN_WIDTH, BF16), sds(ATTN_WIDTH, BF16), sds(ATTN_WIDTH, BF16),
                   jax.ShapeDtypeStruct((s, b * SSM_WIDTH), F32),
                   sds(RET_WIDTH, BF16), sds(RET_WIDTH, BF16), sds(RET_WIDTH, BF16), sds(RET_WIDTH, F32)],
        compiler_params=_cparams("parallel", "arbitrary"),
        name="inproj",
    )(x, sc, sh, g, w_bf16, cos_t, sin_t)


def _attn_body(q_ref, k0_ref, k1_ref, k2_ref, v0_ref, v1_ref, v2_ref, bias_ref, o_ref):
    j = pl.program_id(1)
    tq = q_ref.shape[0]
    col = lax.broadcasted_iota(jnp.int32, (tq, 3 * tq), 1)
    in_seq = col >= (2 - j) * tq
    lane = lax.broadcasted_iota(jnp.int32, (1, LANES), 1)
    low = lane < HEAD_DIM
    for hp in range(ATTN_HEADS // 2):
        sl = slice(hp * LANES, (hp + 1) * LANES)
        q2 = q_ref[:, sl]
        k2 = jnp.concatenate([k0_ref[:, sl], k1_ref[:, sl], k2_ref[:, sl]], axis=0)
        v2 = jnp.concatenate([v0_ref[:, sl], v1_ref[:, sl], v2_ref[:, sl]], axis=0)
        outs = []
        for sub in range(2):
            keep = low if sub == 0 else jnp.logical_not(low)
            kz = jnp.where(keep, k2, jnp.zeros_like(k2))
            s = lax.dot_general(q2, kz, (((1,), (1,)), ((), ())), preferred_element_type=F32)
            s = s + bias_ref[2 * hp + sub]
            s = jnp.where(in_seq, s, NEG_BIG)
            m = jnp.max(s, axis=-1, keepdims=True)
            p = jnp.exp(s - m)
            l = jnp.sum(p, axis=-1, keepdims=True)
            outs.append(jnp.dot(p.astype(BF16), v2, preferred_element_type=F32) / l)
        o_ref[:, sl] = jnp.where(low, outs[0], outs[1]).astype(BF16)


def _attn_bias_table(rel_bias, tq):
    r = np.arange(tq)[:, None]
    c = np.arange(3 * tq)[None, :]
    rel = (c - 2 * tq) - r
    idx = np.clip(rel, -MAX_REL, CHUNK - 1) + MAX_REL
    qc = r // CHUNK + 2 * tq // CHUNK
    kc = c // CHUNK
    band = (kc >= qc - LEFT_CHUNKS) & (kc <= qc)
    table = rel_bias.astype(F32)[:, idx]
    return jnp.where(jnp.asarray(band)[None], table, NEG_BIG)


def _attention(qa, ka, va, bias):
    b, s, w = qa.shape
    tq = ATTN_TQ
    qspec = pl.BlockSpec((None, tq, w), lambda bi, j: (bi, j, 0))
    kspec = lambda back: pl.BlockSpec((None, tq, w), lambda bi, j: (bi, jnp.maximum(j - back, 0), 0))
    return pl.pallas_call(
        _attn_body,
        grid=(b, s // tq),
        in_specs=[qspec, kspec(2), kspec(1), kspec(0), kspec(2), kspec(1), kspec(0),
                  pl.BlockSpec(bias.shape, lambda bi, j: (0, 0, 0))],
        out_specs=qspec,
        out_shape=jax.ShapeDtypeStruct((b, s, w), BF16),
        compiler_params=_cparams("parallel", "arbitrary"),
        name="band_attn",
    )(qa, ka, ka, ka, va, va, va, bias)


def _gelu_tanh(x):
    return 0.5 * x * (1.0 + jnp.tanh(math.sqrt(2.0 / math.pi) * (x + 0.044715 * (x * x * x))))


def _ssm_body(u_ref, bm_ref, cm_ref, ar_ref, ai_ref, d_ref, wg_ref, bg_ref, y_ref, st_ref, bu_ref, yc_ref):
    nb = st_ref.shape[1]
    ts = u_ref.shape[0]
    nc = SSM_COMPLEX
    w = SSM_WIDTH

    @pl.when(pl.program_id(0) == 0)
    def _():
        st_ref[...] = jnp.zeros_like(st_ref)

    n_ct = 2 * nc // LANES
    for bi in range(nb):
        bu = jnp.dot(u_ref[:, bi * w:(bi + 1) * w].astype(BF16), bm_ref[...], preferred_element_type=F32)
        for ct in range(n_ct):
            bu_ref.at[ct][pl.ds(bi, ts, stride=nb), :] = bu[:, ct * LANES:(ct + 1) * LANES]

    tiles_per_block = 4
    half = nc // LANES
    for t0 in range(0, half, tiles_per_block):
        tiles = range(t0, t0 + tiles_per_block)
        ar = [jnp.broadcast_to(ar_ref[:, ct * LANES:(ct + 1) * LANES], (nb, LANES)) for ct in tiles]
        ai = [jnp.broadcast_to(ai_ref[:, ct * LANES:(ct + 1) * LANES], (nb, LANES)) for ct in tiles]

        def step(t, carry):
            rows = pl.ds(pl.multiple_of(t * nb, nb), nb)
            out = []
            for k, ct in enumerate(tiles):
                xr, xi = carry[2 * k], carry[2 * k + 1]
                nr = ar[k] * xr - ai[k] * xi + bu_ref[ct, rows, :]
                ni = ar[k] * xi + ai[k] * xr + bu_ref[half + ct, rows, :]
                bu_ref[ct, rows, :] = nr
                bu_ref[half + ct, rows, :] = ni
                out += [nr, ni]
            return tuple(out)

        init = []
        for ct in tiles:
            init += [st_ref[0, :, ct * LANES:(ct + 1) * LANES], st_ref[1, :, ct * LANES:(ct + 1) * LANES]]
        fin = lax.fori_loop(0, ts, step, tuple(init), unroll=4)
        for k, ct in enumerate(tiles):
            st_ref[0, :, ct * LANES:(ct + 1) * LANES] = fin[2 * k]
            st_ref[1, :, ct * LANES:(ct + 1) * LANES] = fin[2 * k + 1]

    states = jnp.concatenate([bu_ref[ct] for ct in range(n_ct)], axis=1).astype(BF16)
    yc = jnp.dot(states, cm_ref[...], preferred_element_type=F32)
    for ct in range(w // LANES):
        yc_ref[ct] = yc[:, ct * LANES:(ct + 1) * LANES]
    for bi in range(nb):
        cols = slice(bi * w, (bi + 1) * w)
        y_core = jnp.concatenate([yc_ref.at[ct][pl.ds(bi, ts, stride=nb), :] for ct in range(w // LANES)], axis=1)
        y = _gelu_tanh(y_core + d_ref[...] * u_ref[:, cols])
        z = jnp.dot(y.astype(BF16), wg_ref[...], preferred_element_type=F32) + bg_ref[...]
        y_ref[:, cols] = (y * _sigmoid(z)).astype(BF16)


def _ssm_params(a_re, a_im, log_dt, b_re, b_im, c_re, c_im):
    g, p, cg = b_re.shape
    lam = lax.complex(a_re.astype(F32), a_im.astype(F32))
    dt = jnp.exp(log_dt.astype(F32))[:, None]
    a_bar = jnp.exp(lam * dt)
    b_bar = ((a_bar - 1.0) / lam)[:, :, None] * lax.complex(b_re.astype(F32), b_im.astype(F32))
    eye = jnp.eye(g, dtype=F32)
    b_blk = lambda m: jnp.einsum('gpc,gh->gchp', m, eye).reshape(g * cg, g * p)
    bm = jnp.concatenate([b_blk(jnp.real(b_bar)), b_blk(jnp.imag(b_bar))], axis=1)
    c_blk = lambda m: jnp.einsum('gcp,gh->gphc', m.astype(F32), eye).reshape(g * p, g * cg)
    cm = jnp.concatenate([c_blk(c_re), -c_blk(c_im)], axis=0)
    ar = jnp.real(a_bar).reshape(1, g * p)
    ai = jnp.imag(a_bar).reshape(1, g * p)
    return bm.astype(BF16), cm.astype(BF16), ar, ai


def _ssm(us, nb, bm, cm, ar, ai, d_skip, w_glu, b_glu):
    s_len, wide = us.shape
    w = SSM_WIDTH
    ts = SSM_TS
    full = lambda a: pl.BlockSpec(a.shape, lambda i: (0,) * a.ndim)
    d2 = d_skip.reshape(1, w).astype(F32)
    bg2 = b_glu.reshape(1, w).astype(F32)
    wg = w_glu.astype(BF16)
    return pl.pallas_call(
        _ssm_body,
        grid=(s_len // ts,),
        in_specs=[pl.BlockSpec((ts, wide), lambda i: (i, 0)),
                  full(bm), full(cm), full(ar), full(ai), full(d2), full(wg), full(bg2)],
        out_specs=pl.BlockSpec((ts, wide), lambda i: (i, 0)),
        out_shape=jax.ShapeDtypeStruct((s_len, wide), BF16),
        scratch_shapes=[pltpu.VMEM((2, nb, SSM_COMPLEX), F32),
                        pltpu.VMEM((2 * SSM_COMPLEX // LANES, ts * nb, LANES), F32),
                        pltpu.VMEM((w // LANES, ts * nb, LANES), F32)],
        compiler_params=_cparams("arbitrary"),
        name="s5",
    )(us, bm, cm, ar, ai, d2, wg, bg2)


def _ret_body(q_ref, k_ref, v_ref, g_ref, gn_ref, dec_ref, xi_ref, zeta_ref, cd_ref, o_ref, st_ref):
    @pl.when(pl.program_id(1) == 0)
    def _():
        st_ref[...] = jnp.zeros_like(st_ref)

    for h in range(RET_HEADS):
        sl = slice(h * HEAD_DIM, (h + 1) * HEAD_DIM)
        q = q_ref[:, sl]
        k = k_ref[:, sl]
        v = v_ref[:, sl]
        a = lax.dot_general(q, k, (((1,), (1,)), ((), ())), preferred_element_type=F32) * dec_ref[h]
        inner = jnp.dot(a.astype(BF16), v, preferred_element_type=F32)
        st = st_ref[h]
        qx = (q.astype(F32) * xi_ref[h]).astype(BF16)
        y = inner + jnp.dot(qx, st.astype(BF16), preferred_element_type=F32)
        kz = (k.astype(F32) * zeta_ref[h]).astype(BF16)
        kv = lax.dot_general(kz, v, (((0,), (0,)), ((), ())), preferred_element_type=F32)
        st_ref[h] = cd_ref[h] * st + kv
        mu = jnp.mean(y, axis=-1, keepdims=True)
        yc = y - mu
        var = jnp.mean(yc * yc, axis=-1, keepdims=True)
        yn = yc * lax.rsqrt(var + NORM_EPS) * gn_ref[:, sl]
        g = g_ref[:, sl]
        o_ref[:, sl] = (g * _sigmoid(g) * yn).astype(BF16)


def _ret_tables(t_len):
    log_gamma = jnp.log1p(-jnp.exp2(-5.0 - jnp.arange(RET_HEADS, dtype=F32)))
    t = jnp.arange(t_len, dtype=F32)
    diff = t[:, None] - t[None, :]
    dec = jnp.where(diff >= 0, jnp.exp(log_gamma[:, None, None] * jnp.maximum(diff, 0.0)), 0.0)
    bcast = lambda m: jnp.broadcast_to(m[:, :, None], (RET_HEADS, t_len, HEAD_DIM))
    xi = bcast(jnp.exp(log_gamma[:, None] * (t + 1.0)))
    zeta = bcast(jnp.exp(log_gamma[:, None] * (t_len - 1 - t)))
    cd = jnp.broadcast_to(jnp.exp(log_gamma * t_len)[:, None, None], (RET_HEADS, HEAD_DIM, HEAD_DIM))
    return dec, xi, zeta, cd


def _retention(qr, kr, vr, gr, gn_g, tables):
    b, s, w = qr.shape
    t_len = RET_T
    dec, xi, zeta, cd = tables
    tok = pl.BlockSpec((None, t_len, w), lambda bi, j: (bi, j, 0))
    full = lambda a: pl.BlockSpec(a.shape, lambda bi, j: (0,) * a.ndim)
    gn2 = gn_g.reshape(1, w).astype(F32)
    return pl.pallas_call(
        _ret_body,
        grid=(b, s // t_len),
        in_specs=[tok, tok, tok, tok, full(gn2), full(dec), full(xi), full(zeta), full(cd)],
        out_specs=tok,
        out_shape=jax.ShapeDtypeStruct((b, s, w), BF16),
        scratch_shapes=[pltpu.VMEM((RET_HEADS, HEAD_DIM, HEAD_DIM), F32)],
        compiler_params=_cparams("parallel", "arbitrary"),
        name="retention",
    )(qr, kr, vr, gr, gn2, dec, xi, zeta, cd)


def _store_row_tiled(ref, val):
    rows = val.shape[0]
    for t in range(ROW_TILES):
        ref[pl.ds(t, rows, stride=ROW_TILES), :] = val[:, t * LANES:(t + 1) * LANES]


def _load_row_tiled(ref, first, rows):
    return jnp.concatenate(
        [ref[pl.ds(first * ROW_TILES + t, rows, stride=ROW_TILES), :] for t in range(ROW_TILES)], axis=1)


def _outproj_body(x_ref, ya_ref, ys_ref, yr_ref, w_ref, g1_ref, sc_ref, sh_ref, ng_ref, wr_ref, br_ref,
                  xo_ref, h2_ref, route_ref):
    a = ATTN_WIDTH
    mixed = jnp.dot(ya_ref[...], w_ref[0:a, :], preferred_element_type=F32)
    mixed += jnp.dot(ys_ref[...], w_ref[a:a + SSM_WIDTH, :], preferred_element_type=F32)
    mixed += jnp.dot(yr_ref[...], w_ref[a + SSM_WIDTH:, :], preferred_element_type=F32)
    x = x_ref[...] + g1_ref[...] * mixed
    xo_ref[...] = x
    h2 = _modulated_norm(x, ng_ref[...], sc_ref[...], sh_ref[...])
    _store_row_tiled(h2_ref, h2)

    logits = jnp.dot(h2, wr_ref[...], preferred_element_type=F32,
                     precision=lax.Precision.HIGHEST) + br_ref[...]
    lane = lax.broadcasted_iota(jnp.int32, logits.shape, 1)
    big = jnp.int32(1 << 20)

    def first_argmax(vals):
        m = jnp.max(vals, axis=-1, keepdims=True)
        return m, jnp.min(jnp.where(vals == m, lane, big), axis=-1, keepdims=True)

    gl = jnp.where(lane < N_GROUPS, logits, -jnp.inf)
    gmax, gidx = first_argmax(gl)
    group_w = 1.0 / jnp.sum(jnp.exp(gl - gmax), axis=-1, keepdims=True)
    e_lo = N_GROUPS + gidx * EXPERTS_PER_GROUP
    el = jnp.where((lane >= e_lo) & (lane < e_lo + EXPERTS_PER_GROUP), logits, -jnp.inf)
    v1, i1 = first_argmax(el)
    v2, i2 = first_argmax(jnp.where(lane == i1, -jnp.inf, el))
    t2 = jnp.exp(v2 - v1)
    w1 = group_w / (1.0 + t2)
    w2 = group_w * t2 / (1.0 + t2)
    route = jnp.where(lane == 0, (i1 - N_GROUPS).astype(F32),
                      jnp.where(lane == 1, (i2 - N_GROUPS).astype(F32),
                                jnp.where(lane == 2, w1, jnp.where(lane == 3, w2, 0.0))))
    route_ref[...] = route


def _outproj_router(x, ya, ys_tm, yr, w_out_bf16, g1, sc2, sh2, ng, w_route, b_route, ts):
    b, s, d = x.shape
    tok = lambda width: pl.BlockSpec((None, ts, width), lambda bi, j: (bi, j, 0))
    per_b = pl.BlockSpec((None, 1, d), lambda bi, j: (bi, 0, 0))
    full = lambda a: pl.BlockSpec(a.shape, lambda bi, j: (0,) * a.ndim)
    nj = s // ts
    return pl.pallas_call(
        _outproj_body,
        grid=(b, nj),
        in_specs=[tok(d), tok(ATTN_WIDTH),
                  pl.BlockSpec((ts, SSM_WIDTH), lambda bi, j: (j, bi)),
                  tok(RET_WIDTH), full(w_out_bf16), per_b, per_b, per_b, full(ng),
                  full(w_route), full(b_route)],
        out_specs=[tok(d),
                   pl.BlockSpec((ts * ROW_TILES, LANES), lambda bi, j: (bi * nj + j, 0)),
                   pl.BlockSpec((None, ts, LANES), lambda bi, j: (bi, j, 0))],
        out_shape=[jax.ShapeDtypeStruct((b, s, d), F32),
                   jax.ShapeDtypeStruct((b * s * ROW_TILES, LANES), F32),
                   jax.ShapeDtypeStruct((b, s, LANES), F32)],
        compiler_params=_cparams("parallel", "arbitrary"),
        name="outproj_router",
    )(x, ya, ys_tm, yr, w_out_bf16, g1, sc2, sh2, ng, w_route, b_route)


def _slot_plan(expert_ids, n_slots):
    tm = MOE_TM
    e = expert_ids.reshape(-1)
    onehot = (e[:, None] == jnp.arange(N_EXPERTS, dtype=jnp.int32)[None, :]).astype(jnp.int32)
    csum = jnp.cumsum(onehot, axis=0)
    rank = jnp.sum(onehot * (csum - 1), axis=1)
    counts = csum[-1]
    padded = ((counts + tm - 1) // tm) * tm
    ends = jnp.cumsum(padded)
    starts = ends - padded
    dest = (jnp.sum(onehot * starts[None, :], axis=1) + rank).astype(jnp.int32)
    n_tiles = n_slots // tm
    tile_row = jnp.arange(n_tiles, dtype=jnp.int32) * tm
    tile_expert = jnp.minimum(jnp.sum((tile_row[:, None] >= ends[None, :]).astype(jnp.int32), axis=1),
                              N_EXPERTS - 1).astype(jnp.int32)
    used_tiles = (ends[-1] // tm).astype(jnp.int32)
    group_last = jnp.where(padded > 0, ends - tm, n_slots)
    tail = jnp.where(tile_row >= ends[-1], tile_row, n_slots)
    fill = jnp.sort(jnp.concatenate([group_last, tail]).astype(jnp.int32))
    n_fill = jnp.sum(fill < n_slots).astype(jnp.int32)
    meta = jnp.stack([n_fill, used_tiles]).astype(jnp.int32)
    return dest, tile_expert, fill, meta


def _dispatch_body(dest_ref, fill_ref, meta_ref, h2_hbm, xs_hbm, zero_ref, sem, *, pairs_per_step):
    step = pl.program_id(0)
    tile_rows = MOE_TM * ROW_TILES

    def fill_copy(i):
        row = pl.multiple_of(fill_ref[i] * ROW_TILES, tile_rows)
        return pltpu.make_async_copy(zero_ref, xs_hbm.at[pl.ds(row, tile_rows), :], sem)

    @pl.when(step == 0)
    def _():
        zero_ref[...] = jnp.zeros_like(zero_ref)
        n_fill = meta_ref[0]
        lax.fori_loop(0, n_fill, lambda i, c: (fill_copy(i).start(), c)[1], 0)
        lax.fori_loop(0, n_fill, lambda i, c: (fill_copy(i).wait(), c)[1], 0)

    base = step * pairs_per_step

    def row_copy(p):
        src = pl.multiple_of((p // 2) * ROW_TILES, ROW_TILES)
        dst = pl.multiple_of(dest_ref[p] * ROW_TILES, ROW_TILES)
        return pltpu.make_async_copy(h2_hbm.at[pl.ds(src, ROW_TILES), :],
                                     xs_hbm.at[pl.ds(dst, ROW_TILES), :], sem)

    lax.fori_loop(0, pairs_per_step, lambda i, c: (row_copy(base + i).start(), c)[1], 0)
    lax.fori_loop(0, pairs_per_step, lambda i, c: (row_copy(base + i).wait(), c)[1], 0)


def _dispatch(h2_tiled, dest, fill, meta, n_slots):
    n_pairs = dest.shape[0]
    pairs_per_step = 1024
    return pl.pallas_call(
        functools.partial(_dispatch_body, pairs_per_step=pairs_per_step),
        grid_spec=pltpu.PrefetchScalarGridSpec(
            num_scalar_prefetch=3,
            grid=(n_pairs // pairs_per_step,),
            in_specs=[pl.BlockSpec(memory_space=pl.ANY)],
            out_specs=pl.BlockSpec(memory_space=pl.ANY),
            scratch_shapes=[pltpu.VMEM((MOE_TM * ROW_TILES, LANES), F32), pltpu.SemaphoreType.DMA],
        ),
        out_shape=jax.ShapeDtypeStruct((n_slots * ROW_TILES, LANES), F32),
        compiler_params=pltpu.CompilerParams(dimension_semantics=("arbitrary",), has_side_effects=True,
                                             vmem_limit_bytes=VMEM_LIMIT),
        name="moe_dispatch",
    )(dest, fill, meta, h2_tiled)


def _experts_body(te_ref, meta_ref, xs_ref, wg_ref, wu_ref, wd_ref, ys_ref):
    tm = MOE_TM

    @pl.when(pl.program_id(0) < meta_ref[1])
    def _():
        x = _load_row_tiled(xs_ref, 0, tm).astype(BF16)
        hg = jnp.dot(x, wg_ref[...].astype(BF16), preferred_element_type=F32)
        hu = jnp.dot(x, wu_ref[...].astype(BF16), preferred_element_type=F32)
        act = (hg * _sigmoid(hg) * hu).astype(BF16)
        _store_row_tiled(ys_ref, jnp.dot(act, wd_ref[...].astype(BF16), preferred_element_type=F32))

    @pl.when(pl.program_id(0) >= meta_ref[1])
    def _():
        ys_ref[...] = jnp.zeros_like(ys_ref)


def _experts(xs, tile_expert, meta, w_gate, w_up, w_down):
    n_tiles = tile_expert.shape[0]
    rows = MOE_TM * ROW_TILES
    d, f = w_gate.shape[1], w_gate.shape[2]
    return pl.pallas_call(
        _experts_body,
        grid_spec=pltpu.PrefetchScalarGridSpec(
            num_scalar_prefetch=2,
            grid=(n_tiles,),
            in_specs=[pl.BlockSpec((rows, LANES), lambda i, te, mt: (i, 0)),
                      pl.BlockSpec((None, d, f), lambda i, te, mt: (te[i], 0, 0)),
                      pl.BlockSpec((None, d, f), lambda i, te, mt: (te[i], 0, 0)),
                      pl.BlockSpec((None, f, d), lambda i, te, mt: (te[i], 0, 0))],
            out_specs=pl.BlockSpec((rows, LANES), lambda i, te, mt: (i, 0)),
        ),
        out_shape=jax.ShapeDtypeStruct(xs.shape, F32),
        compiler_params=_cparams("arbitrary"),
        name="moe_experts",
    )(tile_expert, meta, xs, w_gate, w_up, w_down)


def _combine_body(dest_ref, x_ref, route_ref, g2_ref, fg_ref, ys_hbm, o_ref, buf_ref, sem, *, final_norm):
    tm = x_ref.shape[0]
    base = (pl.program_id(0) * pl.num_programs(1) + pl.program_id(1)) * tm

    def row_copy(i):
        choice = i // tm
        token = i - choice * tm
        src = pl.multiple_of(dest_ref[(base + token) * 2 + choice] * ROW_TILES, ROW_TILES)
        dst = pl.multiple_of(i * ROW_TILES, ROW_TILES)
        return pltpu.make_async_copy(ys_hbm.at[pl.ds(src, ROW_TILES), :],
                                     buf_ref.at[pl.ds(dst, ROW_TILES), :], sem)

    lax.fori_loop(0, 2 * tm, lambda i, c: (row_copy(i).start(), c)[1], 0)
    lax.fori_loop(0, 2 * tm, lambda i, c: (row_copy(i).wait(), c)[1], 0)
    route = route_ref[...]
    y = route[:, 2:3] * _load_row_tiled(buf_ref, 0, tm) + route[:, 3:4] * _load_row_tiled(buf_ref, tm, tm)
    x = x_ref[...] + g2_ref[...] * y
    if final_norm:
        ms = jnp.mean(x * x, axis=-1, keepdims=True)
        x = x * lax.rsqrt(ms + NORM_EPS) * fg_ref[...]
    o_ref[...] = x


def _combine(x, route, g2, final_g, ys, dest, tm, final_norm):
    b, s, d = x.shape
    tok = lambda width: pl.BlockSpec((None, tm, width), lambda bi, j, dst: (bi, j, 0))
    return pl.pallas_call(
        functools.partial(_combine_body, final_norm=final_norm),
        grid_spec=pltpu.PrefetchScalarGridSpec(
            num_scalar_prefetch=1,
            grid=(b, s // tm),
            in_specs=[tok(d), tok(LANES),
                      pl.BlockSpec((None, 1, d), lambda bi, j, dst: (bi, 0, 0)),
                      pl.BlockSpec((1, d), lambda bi, j, dst: (0, 0)),
                      pl.BlockSpec(memory_space=pl.ANY)],
            out_specs=tok(d),
            scratch_shapes=[pltpu.VMEM((2 * tm * ROW_TILES, LANES), F32), pltpu.SemaphoreType.DMA],
        ),
        out_shape=jax.ShapeDtypeStruct((b, s, d), F32),
        compiler_params=_cparams("arbitrary", "arbitrary"),
        name="moe_combine",
    )(dest, x, route, g2, final_g, ys)


def _rope_tables(s):
    half = HEAD_DIM // 2
    inv_freq = ROPE_BASE ** (-jnp.arange(half, dtype=F32) / half)
    ang = jnp.arange(s, dtype=F32)[:, None] * inv_freq[None, :]
    cos = jnp.tile(jnp.cos(ang), (1, 2 * RET_HEADS))
    sin = jnp.sin(ang)
    sin = jnp.tile(jnp.concatenate([-sin, sin], axis=1), (1, RET_HEADS))
    return cos, sin


def _trunk(x, c, norm1_g, norm2_g, w_ada, b_ada, w_in, attn_rel_bias, ssm_a_re, ssm_a_im, ssm_log_dt,
           ssm_b_re, ssm_b_im, ssm_c_re, ssm_c_im, ssm_d, ssm_w_glu, ssm_b_glu, ret_gn_g, w_out,
           moe_w_group, moe_b_group, moe_w_expert, moe_b_expert, moe_w_gate, moe_w_up, moe_w_down, final_g,
           *, row_tile):
    b, s, d = x.shape
    n_layers = w_in.shape[0]
    n_slots = 2 * b * s + N_EXPERTS * MOE_TM
    mod = _adaln(c, w_ada, b_ada).reshape(n_layers, b, 6, 1, d)
    cos_t, sin_t = _rope_tables(s)
    ret_tables = _ret_tables(RET_T)
    w_in_b = w_in.astype(BF16)
    w_out_b = w_out.astype(BF16)
    n_route = N_GROUPS + N_EXPERTS
    w_route = jnp.pad(jnp.concatenate([moe_w_group, moe_w_expert], axis=-1).astype(F32),
                      ((0, 0), (0, 0), (0, LANES - n_route)))
    b_route = jnp.pad(jnp.concatenate([moe_b_group, moe_b_expert], axis=-1).astype(F32),
                      ((0, 0), (0, LANES - n_route))).reshape(n_layers, 1, LANES)
    fg = final_g.reshape(1, d).astype(F32)
    for i in range(n_layers):
        sh1, sc1, g1, sh2, sc2, g2 = (mod[i, :, k] for k in range(6))
        qa, ka, va, us, qr, kr, vr, gr = _inproj(x, sc1, sh1, norm1_g[i].reshape(1, d), w_in_b[i],
                                                 cos_t, sin_t, row_tile)
        y_a = _attention(qa, ka, va, _attn_bias_table(attn_rel_bias[i], ATTN_TQ))
        bm, cm, ar, ai = _ssm_params(ssm_a_re[i], ssm_a_im[i], ssm_log_dt[i], ssm_b_re[i], ssm_b_im[i],
                                     ssm_c_re[i], ssm_c_im[i])
        y_s = _ssm(us, b, bm, cm, ar, ai, ssm_d[i], ssm_w_glu[i], ssm_b_glu[i])
        y_r = _retention(qr, kr, vr, gr, ret_gn_g[i], ret_tables)
        x, h2, route = _outproj_router(x, y_a, y_s, y_r, w_out_b[i], g1, sc2, sh2,
                                       norm2_g[i].reshape(1, d), w_route[i], b_route[i], row_tile)
        expert_ids = route[:, :, 0:2].astype(jnp.int32).reshape(b * s, 2)
        dest, tile_expert, fill, meta = _slot_plan(expert_ids, n_slots)
        xs = _dispatch(h2, dest, fill, meta, n_slots)
        ys = _experts(xs, tile_expert, meta, moe_w_gate[i], moe_w_up[i], moe_w_down[i])
        x = _combine(x, route, g2, fg, ys, dest, 256, final_norm=(i == n_layers - 1))
    return x


def kernel(x, c, norm1_g, norm2_g, w_ada, b_ada, w_in, attn_rel_bias, ssm_a_re, ssm_a_im, ssm_log_dt, ssm_b_re, ssm_b_im, ssm_c_re, ssm_c_im, ssm_d, ssm_w_glu, ssm_b_glu, ret_gn_g, w_out, moe_w_group, moe_b_group, moe_w_expert, moe_b_expert, moe_w_gate, moe_w_up, moe_w_down, final_g):
    return _trunk(x, c, norm1_g, norm2_g, w_ada, b_ada, w_in, attn_rel_bias, ssm_a_re, ssm_a_im, ssm_log_dt,
                  ssm_b_re, ssm_b_im, ssm_c_re, ssm_c_im, ssm_d, ssm_w_glu, ssm_b_glu, ret_gn_g, w_out,
                  moe_w_group, moe_b_group, moe_w_expert, moe_b_expert, moe_w_gate, moe_w_up, moe_w_down,
                  final_g, row_tile=512)
```

```python
import functools
import math

import jax
import jax.numpy as jnp
import numpy as np
from jax import lax
from jax.experimental import pallas as pl
from jax.experimental.pallas import tpu as pltpu

F32 = jnp.float32
BF16 = jnp.bfloat16

D_MODEL = 1024
N_LAYERS = 4
CHUNK = 64
HEAD_DIM = 64
NORM_EPS = 1e-6

ATTN_HEADS = 8
ATTN_WIDTH = ATTN_HEADS * HEAD_DIM
LEFT_CHUNKS = 8
MAX_REL = 128
REL_TABLE = MAX_REL + CHUNK

SSM_WIDTH = 256
SSM_GROUP = 16
SSM_GROUPS = 16
SSM_STATE = 64
SSM_COMPLEX = SSM_GROUPS * SSM_STATE

RET_HEADS = 4
RET_WIDTH = RET_HEADS * HEAD_DIM
ROPE_BASE = 10000.0

N_GROUPS = 4
EXPERTS_PER_GROUP = 8
N_EXPERTS = N_GROUPS * EXPERTS_PER_GROUP
EXPERT_FF = 256

SUBLANES = 8
LANES = 128
VMEM_LIMIT = 56 * 1024 * 1024

ATTN_TQ = 256
RET_T = 256
SSM_TS = 64
MOE_TM = 256
ROW_TILES = D_MODEL // LANES
NEG_BIG = -1e30


def _sigmoid(x):
    return 1.0 / (1.0 + jnp.exp(-x))


def _cparams(*sem):
    return pltpu.CompilerParams(dimension_semantics=sem, vmem_limit_bytes=VMEM_LIMIT)


def _adaln_body(c_ref, w_ref, b_ref, o_ref):
    c = c_ref[...]
    cond = c * _sigmoid(c)
    o_ref[...] = jnp.dot(cond, w_ref[...], preferred_element_type=F32,
                         precision=lax.Precision.HIGHEST) + b_ref[...]


def _adaln(c, w_ada, b_ada):
    n_l, d, d6 = w_ada.shape
    b = c.shape[0]
    tn = 1536
    return pl.pallas_call(
        _adaln_body,
        grid=(n_l, d6 // tn),
        in_specs=[pl.BlockSpec((b, d), lambda l, j: (0, 0)),
                  pl.BlockSpec((None, d, tn), lambda l, j: (l, 0, j)),
                  pl.BlockSpec((None, 1, tn), lambda l, j: (l, 0, j))],
        out_specs=pl.BlockSpec((None, b, tn), lambda l, j: (l, 0, j)),
        out_shape=jax.ShapeDtypeStruct((n_l, b, d6), F32),
        compiler_params=_cparams("arbitrary", "arbitrary"),
        name="adaln",
    )(c, w_ada, b_ada.reshape(n_l, 1, d6))


def _modulated_norm(x, g, sc, sh):
    ms = jnp.mean(x * x, axis=-1, keepdims=True)
    return (x * lax.rsqrt(ms + NORM_EPS) * g) * (1.0 + sc) + sh


def _inproj_body(x_ref, sc_ref, sh_ref, g_ref, w_ref, cos_ref, sin_ref,
                 qa_ref, ka_ref, va_ref, us_ref, qr_ref, kr_ref, vr_ref, gr_ref):
    hb = _modulated_norm(x_ref[...], g_ref[...], sc_ref[...], sh_ref[...]).astype(BF16)

    def proj(lo, width):
        return jnp.dot(hb, w_ref[:, lo:lo + width], preferred_element_type=F32)

    scale = HEAD_DIM ** -0.5
    a = ATTN_WIDTH
    qa_ref[...] = (proj(0, a) * scale).astype(BF16)
    ka_ref[...] = proj(a, a).astype(BF16)
    va_ref[...] = proj(2 * a, a).astype(BF16)
    o = 3 * a
    us_ref[...] = proj(o, SSM_WIDTH)
    o += SSM_WIDTH

    cos = cos_ref[...]
    sin = sin_ref[...]
    lane = lax.broadcasted_iota(jnp.int32, cos.shape, 1)
    first_half = (lane & (HEAD_DIM // 2)) == 0

    def rotary(z):
        partner = jnp.where(first_half,
                            pltpu.roll(z, RET_WIDTH - HEAD_DIM // 2, 1),
                            pltpu.roll(z, HEAD_DIM // 2, 1))
        return z * cos + partner * sin

    r = RET_WIDTH
    qr_ref[...] = rotary(proj(o, r)).astype(BF16)
    kr_ref[...] = (rotary(proj(o + r, r)) * scale).astype(BF16)
    vr_ref[...] = proj(o + 2 * r, r).astype(BF16)
    gr_ref[...] = proj(o + 3 * r, r)


def _inproj(x, sc, sh, g, w_bf16, cos_t, sin_t, ts):
    b, s, d = x.shape
    nj = s // ts
    tok = lambda width: pl.BlockSpec((None, ts, width), lambda bi, j: (bi, j, 0))
    per_b = pl.BlockSpec((None, 1, d), lambda bi, j: (bi, 0, 0))
    rope = pl.BlockSpec((ts, RET_WIDTH), lambda bi, j: (j, 0))
    sds = lambda width, dt: jax.ShapeDtypeStruct((b, s, width), dt)
    return pl.pallas_call(
        _inproj_body,
        grid=(b, nj),
        in_specs=[tok(d), per_b, per_b,
                  pl.BlockSpec((1, d), lambda bi, j: (0, 0)),
                  pl.BlockSpec(w_bf16.shape, lambda bi, j: (0, 0)),
                  rope, rope],
        out_specs=[tok(ATTN_WIDTH), tok(ATTN_WIDTH), tok(ATTN_WIDTH),
                   pl.BlockSpec((ts, SSM_WIDTH), lambda bi, j: (j, bi)),
                   tok(RET_WIDTH), tok(RET_WIDTH), tok(RET_WIDTH), tok(RET_WIDTH)],
        out_shape=[sds(ATTN_WIDTH, BF16), sds(ATTN_WIDTH, BF16), sds(ATTN_WIDTH, BF16),
                   jax.ShapeDtypeStruct((s, b * SSM_WIDTH), F32),
                   sds(RET_WIDTH, BF16), sds(RET_WIDTH, BF16), sds(RET_WIDTH, BF16), sds(RET_WIDTH, F32)],
        compiler_params=_cparams("parallel", "arbitrary"),
        name="inproj",
    )(x, sc, sh, g, w_bf16, cos_t, sin_t)


def _attn_body(q_ref, k0_ref, k1_ref, k2_ref, v0_ref, v1_ref, v2_ref, bias_ref, o_ref):
    j = pl.program_id(1)
    tq = q_ref.shape[0]
    col = lax.broadcasted_iota(jnp.int32, (tq, 3 * tq), 1)
    in_seq = col >= (2 - j) * tq
    lane = lax.broadcasted_iota(jnp.int32, (1, LANES), 1)
    low = lane < HEAD_DIM
    for hp in range(ATTN_HEADS // 2):
        sl = slice(hp * LANES, (hp + 1) * LANES)
        q2 = q_ref[:, sl]
        k2 = jnp.concatenate([k0_ref[:, sl], k1_ref[:, sl], k2_ref[:, sl]], axis=0)
        v2 = jnp.concatenate([v0_ref[:, sl], v1_ref[:, sl], v2_ref[:, sl]], axis=0)
        outs = []
        for sub in range(2):
            keep = low if sub == 0 else jnp.logical_not(low)
            kz = jnp.where(keep, k2, jnp.zeros_like(k2))
            s = lax.dot_general(q2, kz, (((1,), (1,)), ((), ())), preferred_element_type=F32)
            s = s + bias_ref[2 * hp + sub]
            s = jnp.where(in_seq, s, NEG_BIG)
            m = jnp.max(s, axis=-1, keepdims=True)
            p = jnp.exp(s - m)
            l = jnp.sum(p, axis=-1, keepdims=True)
            outs.append(jnp.dot(p.astype(BF16), v2, preferred_element_type=F32) / l)
        o_ref[:, sl] = jnp.where(low, outs[0], outs[1]).astype(BF16)


def _attn_bias_table(rel_bias, tq):
    r = np.arange(tq)[:, None]
    c = np.arange(3 * tq)[None, :]
    qc = r // CHUNK + 2 * tq // CHUNK
    kc = c // CHUNK
    band = (kc >= qc - LEFT_CHUNKS) & (kc <= qc)
    period = 4 * tq
    k = np.arange(period)
    diff = np.where(k < 3 * tq, k, k - period)
    idx = np.clip(diff - 2 * tq, -MAX_REL, CHUNK - 1) + MAX_REL
    vec = rel_bias.astype(F32)[:, idx]
    h = vec.shape[0]
    table = jnp.tile(vec, (1, tq))[:, :tq * (period - 1)].reshape(h, tq, period - 1)[:, :, :3 * tq]
    return jnp.where(jnp.asarray(band)[None], table, NEG_BIG)


def _attention(qa, ka, va, bias):
    b, s, w = qa.shape
    tq = ATTN_TQ
    qspec = pl.BlockSpec((None, tq, w), lambda bi, j: (bi, j, 0))
    kspec = lambda back: pl.BlockSpec((None, tq, w), lambda bi, j: (bi, jnp.maximum(j - back, 0), 0))
    return pl.pallas_call(
        _attn_body,
        grid=(b, s // tq),
        in_specs=[qspec, kspec(2), kspec(1), kspec(0), kspec(2), kspec(1), kspec(0),
                  pl.BlockSpec(bias.shape, lambda bi, j: (0, 0, 0))],
        out_specs=qspec,
        out_shape=jax.ShapeDtypeStruct((b, s, w), BF16),
        compiler_params=_cparams("parallel", "arbitrary"),
        name="band_attn",
    )(qa, ka, ka, ka, va, va, va, bias)


def _gelu_tanh(x):
    return 0.5 * x * (1.0 + jnp.tanh(math.sqrt(2.0 / math.pi) * (x + 0.044715 * (x * x * x))))


def _ssm_body(u_ref, bm_ref, cm_ref, ar_ref, ai_ref, d_ref, wg_ref, bg_ref, y_ref, st_ref, bu_ref, yc_ref):
    nb = st_ref.shape[1]
    ts = u_ref.shape[0]
    nc = SSM_COMPLEX
    w = SSM_WIDTH

    @pl.when(pl.program_id(0) == 0)
    def _():
        st_ref[...] = jnp.zeros_like(st_ref)

    n_ct = 2 * nc // LANES
    for bi in range(nb):
        bu = jnp.dot(u_ref[:, bi * w:(bi + 1) * w].astype(BF16), bm_ref[...], preferred_element_type=F32)
        for ct in range(n_ct):
            bu_ref.at[ct][pl.ds(bi, ts, stride=nb), :] = bu[:, ct * LANES:(ct + 1) * LANES]

    tiles_per_block = 4
    half = nc // LANES
    for t0 in range(0, half, tiles_per_block):
        tiles = range(t0, t0 + tiles_per_block)
        ar = [jnp.broadcast_to(ar_ref[:, ct * LANES:(ct + 1) * LANES], (nb, LANES)) for ct in tiles]
        ai = [jnp.broadcast_to(ai_ref[:, ct * LANES:(ct + 1) * LANES], (nb, LANES)) for ct in tiles]

        def step(t, carry):
            rows = pl.ds(pl.multiple_of(t * nb, nb), nb)
            out = []
            for k, ct in enumerate(tiles):
                xr, xi = carry[2 * k], carry[2 * k + 1]
                nr = ar[k] * xr - ai[k] * xi + bu_ref[ct, rows, :]
                ni = ar[k] * xi + ai[k] * xr + bu_ref[half + ct, rows, :]
                bu_ref[ct, rows, :] = nr
                bu_ref[half + ct, rows, :] = ni
                out += [nr, ni]
            return tuple(out)

        init = []
        for ct in tiles:
            init += [st_ref[0, :, ct * LANES:(ct + 1) * LANES], st_ref[1, :, ct * LANES:(ct + 1) * LANES]]
        fin = lax.fori_loop(0, ts, step, tuple(init), unroll=4)
        for k, ct in enumerate(tiles):
            st_ref[0, :, ct * LANES:(ct + 1) * LANES] = fin[2 * k]
            st_ref[1, :, ct * LANES:(ct + 1) * LANES] = fin[2 * k + 1]

    states = jnp.concatenate([bu_ref[ct] for ct in range(n_ct)], axis=1).astype(BF16)
    yc = jnp.dot(states, cm_ref[...], preferred_element_type=F32)
    for ct in range(w // LANES):
        yc_ref[ct] = yc[:, ct * LANES:(ct + 1) * LANES]
    for bi in range(nb):
        cols = slice(bi * w, (bi + 1) * w)
        y_core = jnp.concatenate([yc_ref.at[ct][pl.ds(bi, ts, stride=nb), :] for ct in range(w // LANES)], axis=1)
        y = _gelu_tanh(y_core + d_ref[...] * u_ref[:, cols])
        z = jnp.dot(y.astype(BF16), wg_ref[...], preferred_element_type=F32) + bg_ref[...]
        y_ref[:, cols] = (y * _sigmoid(z)).astype(BF16)


def _ssm_params(a_re, a_im, log_dt, b_re, b_im, c_re, c_im):
    g, p, cg = b_re.shape
    lam = lax.complex(a_re.astype(F32), a_im.astype(F32))
    dt = jnp.exp(log_dt.astype(F32))[:, None]
    a_bar = jnp.exp(lam * dt)
    b_bar = ((a_bar - 1.0) / lam)[:, :, None] * lax.complex(b_re.astype(F32), b_im.astype(F32))
    eye = jnp.eye(g, dtype=F32)
    b_blk = lambda m: jnp.einsum('gpc,gh->gchp', m, eye).reshape(g * cg, g * p)
    bm = jnp.concatenate([b_blk(jnp.real(b_bar)), b_blk(jnp.imag(b_bar))], axis=1)
    c_blk = lambda m: jnp.einsum('gcp,gh->gphc', m.astype(F32), eye).reshape(g * p, g * cg)
    cm = jnp.concatenate([c_blk(c_re), -c_blk(c_im)], axis=0)
    ar = jnp.real(a_bar).reshape(1, g * p)
    ai = jnp.imag(a_bar).reshape(1, g * p)
    return bm.astype(BF16), cm.astype(BF16), ar, ai


def _ssm(us, nb, bm, cm, ar, ai, d_skip, w_glu, b_glu):
    s_len, wide = us.shape
    w = SSM_WIDTH
    ts = SSM_TS
    full = lambda a: pl.BlockSpec(a.shape, lambda i: (0,) * a.ndim)
    d2 = d_skip.reshape(1, w).astype(F32)
    bg2 = b_glu.reshape(1, w).astype(F32)
    wg = w_glu.astype(BF16)
    return pl.pallas_call(
        _ssm_body,
        grid=(s_len // ts,),
        in_specs=[pl.BlockSpec((ts, wide), lambda i: (i, 0)),
                  full(bm), full(cm), full(ar), full(ai), full(d2), full(wg), full(bg2)],
        out_specs=pl.BlockSpec((ts, wide), lambda i: (i, 0)),
        out_shape=jax.ShapeDtypeStruct((s_len, wide), BF16),
        scratch_shapes=[pltpu.VMEM((2, nb, SSM_COMPLEX), F32),
                        pltpu.VMEM((2 * SSM_COMPLEX // LANES, ts * nb, LANES), F32),
                        pltpu.VMEM((w // LANES, ts * nb, LANES), F32)],
        compiler_params=_cparams("arbitrary"),
        name="s5",
    )(us, bm, cm, ar, ai, d2, wg, bg2)


def _ret_body(q_ref, k_ref, v_ref, g_ref, gn_ref, dec_ref, xi_ref, zeta_ref, cd_ref, o_ref, st_ref):
    @pl.when(pl.program_id(1) == 0)
    def _():
        st_ref[...] = jnp.zeros_like(st_ref)

    for h in range(RET_HEADS):
        sl = slice(h * HEAD_DIM, (h + 1) * HEAD_DIM)
        q = q_ref[:, sl]
        k = k_ref[:, sl]
        v = v_ref[:, sl]
        a = lax.dot_general(q, k, (((1,), (1,)), ((), ())), preferred_element_type=F32) * dec_ref[h]
        inner = jnp.dot(a.astype(BF16), v, preferred_element_type=F32)
        st = st_ref[h]
        qx = (q.astype(F32) * xi_ref[h]).astype(BF16)
        y = inner + jnp.dot(qx, st.astype(BF16), preferred_element_type=F32)
        kz = (k.astype(F32) * zeta_ref[h]).astype(BF16)
        kv = lax.dot_general(kz, v, (((0,), (0,)), ((), ())), preferred_element_type=F32)
        st_ref[h] = cd_ref[h] * st + kv
        mu = jnp.mean(y, axis=-1, keepdims=True)
        yc = y - mu
        var = jnp.mean(yc * yc, axis=-1, keepdims=True)
        yn = yc * lax.rsqrt(var + NORM_EPS) * gn_ref[:, sl]
        g = g_ref[:, sl]
        o_ref[:, sl] = (g * _sigmoid(g) * yn).astype(BF16)


def _ret_tables(t_len):
    log_gamma = jnp.log1p(-jnp.exp2(-5.0 - jnp.arange(RET_HEADS, dtype=F32)))
    t = jnp.arange(t_len, dtype=F32)
    diff = t[:, None] - t[None, :]
    dec = jnp.where(diff >= 0, jnp.exp(log_gamma[:, None, None] * jnp.maximum(diff, 0.0)), 0.0)
    bcast = lambda m: jnp.broadcast_to(m[:, :, None], (RET_HEADS, t_len, HEAD_DIM))
    xi = bcast(jnp.exp(log_gamma[:, None] * (t + 1.0)))
    zeta = bcast(jnp.exp(log_gamma[:, None] * (t_len - 1 - t)))
    cd = jnp.broadcast_to(jnp.exp(log_gamma * t_len)[:, None, None], (RET_HEADS, HEAD_DIM, HEAD_DIM))
    return dec, xi, zeta, cd


def _retention(qr, kr, vr, gr, gn_g, tables):
    b, s, w = qr.shape
    t_len = RET_T
    dec, xi, zeta, cd = tables
    tok = pl.BlockSpec((None, t_len, w), lambda bi, j: (bi, j, 0))
    full = lambda a: pl.BlockSpec(a.shape, lambda bi, j: (0,) * a.ndim)
    gn2 = gn_g.reshape(1, w).astype(F32)
    return pl.pallas_call(
        _ret_body,
        grid=(b, s // t_len),
        in_specs=[tok, tok, tok, tok, full(gn2), full(dec), full(xi), full(zeta), full(cd)],
        out_specs=tok,
        out_shape=jax.ShapeDtypeStruct((b, s, w), BF16),
        scratch_shapes=[pltpu.VMEM((RET_HEADS, HEAD_DIM, HEAD_DIM), F32)],
        compiler_params=_cparams("parallel", "arbitrary"),
        name="retention",
    )(qr, kr, vr, gr, gn2, dec, xi, zeta, cd)


def _store_row_tiled(ref, val):
    rows = val.shape[0]
    for t in range(ROW_TILES):
        ref[pl.ds(t, rows, stride=ROW_TILES), :] = val[:, t * LANES:(t + 1) * LANES]


def _load_row_tiled(ref, first, rows):
    return jnp.concatenate(
        [ref[pl.ds(first * ROW_TILES + t, rows, stride=ROW_TILES), :] for t in range(ROW_TILES)], axis=1)


def _outproj_body(x_ref, ya_ref, ys_ref, yr_ref, w_ref, g1_ref, sc_ref, sh_ref, ng_ref, wr_ref, br_ref,
                  xo_ref, h2_ref, route_ref):
    a = ATTN_WIDTH
    mixed = jnp.dot(ya_ref[...], w_ref[0:a, :], preferred_element_type=F32)
    mixed += jnp.dot(ys_ref[...], w_ref[a:a + SSM_WIDTH, :], preferred_element_type=F32)
    mixed += jnp.dot(yr_ref[...], w_ref[a + SSM_WIDTH:, :], preferred_element_type=F32)
    x = x_ref[...] + g1_ref[...] * mixed
    xo_ref[...] = x
    h2 = _modulated_norm(x, ng_ref[...], sc_ref[...], sh_ref[...])
    _store_row_tiled(h2_ref, h2)

    h_hi = h2.astype(BF16)
    h_lo = (h2 - h_hi.astype(F32)).astype(BF16)
    logits = (jnp.dot(h_hi, wr_ref[0], preferred_element_type=F32)
              + (jnp.dot(h_hi, wr_ref[1], preferred_element_type=F32)
                 + jnp.dot(h_lo, wr_ref[0], preferred_element_type=F32))
              + br_ref[...])
    lane = lax.broadcasted_iota(jnp.int32, logits.shape, 1).astype(F32)
    big = float(LANES)

    def first_argmax(vals):
        m = jnp.max(vals, axis=-1, keepdims=True)
        return m, jnp.min(jnp.where(vals == m, lane, big), axis=-1, keepdims=True)

    gl = jnp.where(lane < N_GROUPS, logits, -jnp.inf)
    gmax, gidx = first_argmax(gl)
    group_w = 1.0 / jnp.sum(jnp.exp(gl - gmax), axis=-1, keepdims=True)
    e_lo = N_GROUPS + gidx * EXPERTS_PER_GROUP
    el = jnp.where((lane >= e_lo) & (lane < e_lo + EXPERTS_PER_GROUP), logits, -jnp.inf)
    v1, i1 = first_argmax(el)
    v2, i2 = first_argmax(jnp.where(lane == i1, -jnp.inf, el))
    t2 = jnp.exp(v2 - v1)
    w1 = group_w / (1.0 + t2)
    w2 = group_w * t2 / (1.0 + t2)
    route = jnp.where(lane == 0, i1 - N_GROUPS,
                      jnp.where(lane == 1, i2 - N_GROUPS,
                                jnp.where(lane == 2, w1, jnp.where(lane == 3, w2, 0.0))))
    route_ref[...] = route


def _outproj_router(x, ya, ys_tm, yr, w_out_bf16, g1, sc2, sh2, ng, w_route, b_route, ts):
    b, s, d = x.shape
    tok = lambda width: pl.BlockSpec((None, ts, width), lambda bi, j: (bi, j, 0))
    per_b = pl.BlockSpec((None, 1, d), lambda bi, j: (bi, 0, 0))
    full = lambda a: pl.BlockSpec(a.shape, lambda bi, j: (0,) * a.ndim)
    nj = s // ts
    return pl.pallas_call(
        _outproj_body,
        grid=(b, nj),
        in_specs=[tok(d), tok(ATTN_WIDTH),
                  pl.BlockSpec((ts, SSM_WIDTH), lambda bi, j: (j, bi)),
                  tok(RET_WIDTH), full(w_out_bf16), per_b, per_b, per_b, full(ng),
                  full(w_route), full(b_route)],
        out_specs=[tok(d),
                   pl.BlockSpec((ts * ROW_TILES, LANES), lambda bi, j: (bi * nj + j, 0)),
                   pl.BlockSpec((None, ts, LANES), lambda bi, j: (bi, j, 0))],
        out_shape=[jax.ShapeDtypeStruct((b, s, d), F32),
                   jax.ShapeDtypeStruct((b * s * ROW_TILES, LANES), F32),
                   jax.ShapeDtypeStruct((b, s, LANES), F32)],
        compiler_params=_cparams("parallel", "arbitrary"),
        name="outproj_router",
    )(x, ya, ys_tm, yr, w_out_bf16, g1, sc2, sh2, ng, w_route, b_route)


def _slot_plan(expert_ids, n_slots):
    tm = MOE_TM
    e = expert_ids.reshape(-1)
    onehot = (e[:, None] == jnp.arange(N_EXPERTS, dtype=jnp.int32)[None, :]).astype(jnp.int32)
    csum = jnp.cumsum(onehot, axis=0)
    rank = jnp.sum(onehot * (csum - 1), axis=1)
    counts = csum[-1]
    padded = ((counts + tm - 1) // tm) * tm
    ends = jnp.cumsum(padded)
    starts = ends - padded
    dest = (jnp.sum(onehot * starts[None, :], axis=1) + rank).astype(jnp.int32)
    n_tiles = n_slots // tm
    tile_row = jnp.arange(n_tiles, dtype=jnp.int32) * tm
    tile_expert = jnp.minimum(jnp.sum((tile_row[:, None] >= ends[None, :]).astype(jnp.int32), axis=1),
                              N_EXPERTS - 1).astype(jnp.int32)
    used_tiles = (ends[-1] // tm).astype(jnp.int32)
    group_last = jnp.where(padded > 0, ends - tm, n_slots)
    tail = jnp.where(tile_row >= ends[-1], tile_row, n_slots)
    fill = jnp.sort(jnp.concatenate([group_last, tail]).astype(jnp.int32))
    n_fill = jnp.sum(fill < n_slots).astype(jnp.int32)
    meta = jnp.stack([n_fill, used_tiles]).astype(jnp.int32)
    return dest, tile_expert, fill, meta


DMA_UNROLL = 8


def _dispatch_body(dest_ref, fill_ref, meta_ref, h2_ref, xs_hbm, zero_ref, sem):
    step = pl.program_id(0)
    tile_rows = MOE_TM * ROW_TILES
    tokens = h2_ref.shape[0] // ROW_TILES

    def fill_copy(i):
        row = pl.multiple_of(fill_ref[i] * ROW_TILES, tile_rows)
        return pltpu.make_async_copy(zero_ref, xs_hbm.at[pl.ds(row, tile_rows), :], sem)

    @pl.when(step == 0)
    def _():
        zero_ref[...] = jnp.zeros_like(zero_ref)
        n_fill = meta_ref[0]
        lax.fori_loop(0, n_fill, lambda i, c: (fill_copy(i).start(), c)[1], 0)
        lax.fori_loop(0, n_fill, lambda i, c: (fill_copy(i).wait(), c)[1], 0)

    base = step * (2 * tokens)

    def row_copy(i, choice):
        src = pl.multiple_of(i * ROW_TILES, ROW_TILES)
        dst = pl.multiple_of(dest_ref[base + 2 * i + choice] * ROW_TILES, ROW_TILES)
        return pltpu.make_async_copy(h2_ref.at[pl.ds(src, ROW_TILES), :],
                                     xs_hbm.at[pl.ds(dst, ROW_TILES), :], sem)

    def start(i, c):
        row_copy(i, 0).start()
        row_copy(i, 1).start()
        return c

    def wait(i, c):
        row_copy(i, 0).wait()
        row_copy(i, 1).wait()
        return c

    lax.fori_loop(0, tokens, start, 0, unroll=DMA_UNROLL)
    lax.fori_loop(0, tokens, wait, 0, unroll=DMA_UNROLL)


def _dispatch(h2_tiled, dest, fill, meta, n_slots):
    tokens = 512
    rows = tokens * ROW_TILES
    return pl.pallas_call(
        _dispatch_body,
        grid_spec=pltpu.PrefetchScalarGridSpec(
            num_scalar_prefetch=3,
            grid=(h2_tiled.shape[0] // rows,),
            in_specs=[pl.BlockSpec((rows, LANES), lambda i, d, f, m: (i, 0))],
            out_specs=pl.BlockSpec(memory_space=pl.ANY),
            scratch_shapes=[pltpu.VMEM((MOE_TM * ROW_TILES, LANES), F32), pltpu.SemaphoreType.DMA],
        ),
        out_shape=jax.ShapeDtypeStruct((n_slots * ROW_TILES, LANES), F32),
        compiler_params=pltpu.CompilerParams(dimension_semantics=("arbitrary",), has_side_effects=True,
                                             vmem_limit_bytes=VMEM_LIMIT),
        name="moe_dispatch",
    )(dest, fill, meta, h2_tiled)


def _experts_body(te_ref, meta_ref, xs_ref, wg_ref, wu_ref, wd_ref, ys_ref):
    tm = MOE_TM

    @pl.when(pl.program_id(0) < meta_ref[1])
    def _():
        x = _load_row_tiled(xs_ref, 0, tm).astype(BF16)
        hg = jnp.dot(x, wg_ref[...].astype(BF16), preferred_element_type=F32)
        hu = jnp.dot(x, wu_ref[...].astype(BF16), preferred_element_type=F32)
        act = (hg * _sigmoid(hg) * hu).astype(BF16)
        _store_row_tiled(ys_ref, jnp.dot(act, wd_ref[...].astype(BF16), preferred_element_type=F32))

    @pl.when(pl.program_id(0) >= meta_ref[1])
    def _():
        ys_ref[...] = jnp.zeros_like(ys_ref)


def _experts(xs, tile_expert, meta, w_gate, w_up, w_down):
    n_tiles = tile_expert.shape[0]
    rows = MOE_TM * ROW_TILES
    d, f = w_gate.shape[1], w_gate.shape[2]
    return pl.pallas_call(
        _experts_body,
        grid_spec=pltpu.PrefetchScalarGridSpec(
            num_scalar_prefetch=2,
            grid=(n_tiles,),
            in_specs=[pl.BlockSpec((rows, LANES), lambda i, te, mt: (i, 0)),
                      pl.BlockSpec((None, d, f), lambda i, te, mt: (te[i], 0, 0)),
                      pl.BlockSpec((None, d, f), lambda i, te, mt: (te[i], 0, 0)),
                      pl.BlockSpec((None, f, d), lambda i, te, mt: (te[i], 0, 0))],
            out_specs=pl.BlockSpec((rows, LANES), lambda i, te, mt: (i, 0)),
        ),
        out_shape=jax.ShapeDtypeStruct(xs.shape, F32),
        compiler_params=_cparams("arbitrary"),
        name="moe_experts",
    )(tile_expert, meta, xs, w_gate, w_up, w_down)


def _combine_body(dest_ref, x_ref, route_ref, g2_ref, fg_ref, ys_hbm, o_ref, buf_ref, sem, *, final_norm):
    tm = x_ref.shape[0]
    base = (pl.program_id(0) * pl.num_programs(1) + pl.program_id(1)) * tm

    def row_copy(i, choice):
        src = pl.multiple_of(dest_ref[(base + i) * 2 + choice] * ROW_TILES, ROW_TILES)
        dst = pl.multiple_of((choice * tm + i) * ROW_TILES, ROW_TILES)
        return pltpu.make_async_copy(ys_hbm.at[pl.ds(src, ROW_TILES), :],
                                     buf_ref.at[pl.ds(dst, ROW_TILES), :], sem)

    def start(i, c):
        row_copy(i, 0).start()
        row_copy(i, 1).start()
        return c

    def wait(i, c):
        row_copy(i, 0).wait()
        row_copy(i, 1).wait()
        return c

    lax.fori_loop(0, tm, start, 0, unroll=DMA_UNROLL)
    lax.fori_loop(0, tm, wait, 0, unroll=DMA_UNROLL)
    route = route_ref[...]
    y = route[:, 2:3] * _load_row_tiled(buf_ref, 0, tm) + route[:, 3:4] * _load_row_tiled(buf_ref, tm, tm)
    x = x_ref[...] + g2_ref[...] * y
    if final_norm:
        ms = jnp.mean(x * x, axis=-1, keepdims=True)
        x = x * lax.rsqrt(ms + NORM_EPS) * fg_ref[...]
    o_ref[...] = x


def _combine(x, route, g2, final_g, ys, dest, tm, final_norm):
    b, s, d = x.shape
    tok = lambda width: pl.BlockSpec((None, tm, width), lambda bi, j, dst: (bi, j, 0))
    return pl.pallas_call(
        functools.partial(_combine_body, final_norm=final_norm),
        grid_spec=pltpu.PrefetchScalarGridSpec(
            num_scalar_prefetch=1,
            grid=(b, s // tm),
            in_specs=[tok(d), tok(LANES),
                      pl.BlockSpec((None, 1, d), lambda bi, j, dst: (bi, 0, 0)),
                      pl.BlockSpec((1, d), lambda bi, j, dst: (0, 0)),
                      pl.BlockSpec(memory_space=pl.ANY)],
            out_specs=tok(d),
            scratch_shapes=[pltpu.VMEM((2 * tm * ROW_TILES, LANES), F32), pltpu.SemaphoreType.DMA],
        ),
        out_shape=jax.ShapeDtypeStruct((b, s, d), F32),
        compiler_params=_cparams("arbitrary", "arbitrary"),
        name="moe_combine",
    )(dest, x, route, g2, final_g, ys)


def _rope_tables(s):
    half = HEAD_DIM // 2
    inv_freq = ROPE_BASE ** (-jnp.arange(half, dtype=F32) / half)
    ang = jnp.arange(s, dtype=F32)[:, None] * inv_freq[None, :]
    cos = jnp.tile(jnp.cos(ang), (1, 2 * RET_HEADS))
    sin = jnp.sin(ang)
    sin = jnp.tile(jnp.concatenate([-sin, sin], axis=1), (1, RET_HEADS))
    return cos, sin


def _trunk(x, c, norm1_g, norm2_g, w_ada, b_ada, w_in, attn_rel_bias, ssm_a_re, ssm_a_im, ssm_log_dt,
           ssm_b_re, ssm_b_im, ssm_c_re, ssm_c_im, ssm_d, ssm_w_glu, ssm_b_glu, ret_gn_g, w_out,
           moe_w_group, moe_b_group, moe_w_expert, moe_b_expert, moe_w_gate, moe_w_up, moe_w_down, final_g,
           *, row_tile):
    b, s, d = x.shape
    n_layers = w_in.shape[0]
    n_slots = 2 * b * s + N_EXPERTS * MOE_TM
    mod = _adaln(c, w_ada, b_ada).reshape(n_layers, b, 6, 1, d)
    cos_t, sin_t = _rope_tables(s)
    ret_tables = _ret_tables(RET_T)
    w_in_b = w_in.astype(BF16)
    w_out_b = w_out.astype(BF16)
    n_route = N_GROUPS + N_EXPERTS
    w_route = jnp.pad(jnp.concatenate([moe_w_group, moe_w_expert], axis=-1).astype(F32),
                      ((0, 0), (0, 0), (0, LANES - n_route)))
    w_route_hi = w_route.astype(BF16)
    w_route = jnp.stack([w_route_hi, (w_route - w_route_hi.astype(F32)).astype(BF16)], axis=1)
    b_route = jnp.pad(jnp.concatenate([moe_b_group, moe_b_expert], axis=-1).astype(F32),
                      ((0, 0), (0, LANES - n_route))).reshape(n_layers, 1, LANES)
    fg = final_g.reshape(1, d).astype(F32)
    for i in range(n_layers):
        sh1, sc1, g1, sh2, sc2, g2 = (mod[i, :, k] for k in range(6))
        qa, ka, va, us, qr, kr, vr, gr = _inproj(x, sc1, sh1, norm1_g[i].reshape(1, d), w_in_b[i],
                                                 cos_t, sin_t, row_tile)
        y_a = _attention(qa, ka, va, _attn_bias_table(attn_rel_bias[i], ATTN_TQ))
        bm, cm, ar, ai = _ssm_params(ssm_a_re[i], ssm_a_im[i], ssm_log_dt[i], ssm_b_re[i], ssm_b_im[i],
                                     ssm_c_re[i], ssm_c_im[i])
        y_s = _ssm(us, b, bm, cm, ar, ai, ssm_d[i], ssm_w_glu[i], ssm_b_glu[i])
        y_r = _retention(qr, kr, vr, gr, ret_gn_g[i], ret_tables)
        x, h2, route = _outproj_router(x, y_a, y_s, y_r, w_out_b[i], g1, sc2, sh2,
                                       norm2_g[i].reshape(1, d), w_route[i], b_route[i], row_tile)
        expert_ids = route[:, :, 0:2].astype(jnp.int32).reshape(b * s, 2)
        dest, tile_expert, fill, meta = _slot_plan(expert_ids, n_slots)
        xs = _dispatch(h2, dest, fill, meta, n_slots)
        ys = _experts(xs, tile_expert, meta, moe_w_gate[i], moe_w_up[i], moe_w_down[i])
        x = _combine(x, route, g2, fg, ys, dest, 256, final_norm=(i == n_layers - 1))
    return x


def kernel(x, c, norm1_g, norm2_g, w_ada, b_ada, w_in, attn_rel_bias, ssm_a_re, ssm_a_im, ssm_log_dt, ssm_b_re, ssm_b_im, ssm_c_re, ssm_c_im, ssm_d, ssm_w_glu, ssm_b_glu, ret_gn_g, w_out, moe_w_group, moe_b_group, moe_w_expert, moe_b_expert, moe_w_gate, moe_w_up, moe_w_down, final_g):
    return _trunk(x, c, norm1_g, norm2_g, w_ada, b_ada, w_in, attn_rel_bias, ssm_a_re, ssm_a_im, ssm_log_dt,
                  ssm_b_re, ssm_b_im, ssm_c_re, ssm_c_im, ssm_d, ssm_w_glu, ssm_b_glu, ret_gn_g, w_out,
                  moe_w_group, moe_b_group, moe_w_expert, moe_b_expert, moe_w_gate, moe_w_up, moe_w_down,
                  final_g, row_tile=512)
```

```python
import functools
import math

import jax
import jax.numpy as jnp
import numpy as np
from jax import lax
from jax.experimental import pallas as pl
from jax.experimental.pallas import tpu as pltpu

F32 = jnp.float32
BF16 = jnp.bfloat16

D_MODEL = 1024
N_LAYERS = 4
CHUNK = 64
HEAD_DIM = 64
NORM_EPS = 1e-6

ATTN_HEADS = 8
ATTN_WIDTH = ATTN_HEADS * HEAD_DIM
LEFT_CHUNKS = 8
MAX_REL = 128
REL_TABLE = MAX_REL + CHUNK

SSM_WIDTH = 256
SSM_GROUP = 16
SSM_GROUPS = 16
SSM_STATE = 64
SSM_COMPLEX = SSM_GROUPS * SSM_STATE

RET_HEADS = 4
RET_WIDTH = RET_HEADS * HEAD_DIM
ROPE_BASE = 10000.0

N_GROUPS = 4
EXPERTS_PER_GROUP = 8
N_EXPERTS = N_GROUPS * EXPERTS_PER_GROUP
EXPERT_FF = 256

SUBLANES = 8
LANES = 128
VMEM_LIMIT = 56 * 1024 * 1024

ATTN_TQ = 256
RET_T = 256
SSM_TS = 64
MOE_TM = 512
ROW_TILES = D_MODEL // LANES
NEG_BIG = -1e30


def _sigmoid(x):
    return 1.0 / (1.0 + jnp.exp(-x))


def _cparams(*sem):
    return pltpu.CompilerParams(dimension_semantics=sem, vmem_limit_bytes=VMEM_LIMIT)


def _adaln_body(c_ref, w_ref, b_ref, o_ref):
    c = c_ref[...]
    cond = c * _sigmoid(c)
    o_ref[...] = jnp.dot(cond, w_ref[...], preferred_element_type=F32,
                         precision=lax.Precision.HIGHEST) + b_ref[...]


def _adaln(c, w_ada, b_ada):
    n_l, d, d6 = w_ada.shape
    b = c.shape[0]
    tn = 1536
    return pl.pallas_call(
        _adaln_body,
        grid=(n_l, d6 // tn),
        in_specs=[pl.BlockSpec((b, d), lambda l, j: (0, 0)),
                  pl.BlockSpec((None, d, tn), lambda l, j: (l, 0, j)),
                  pl.BlockSpec((None, 1, tn), lambda l, j: (l, 0, j))],
        out_specs=pl.BlockSpec((None, b, tn), lambda l, j: (l, 0, j)),
        out_shape=jax.ShapeDtypeStruct((n_l, b, d6), F32),
        compiler_params=_cparams("arbitrary", "arbitrary"),
        name="adaln",
    )(c, w_ada, b_ada.reshape(n_l, 1, d6))


def _modulated_norm(x, g, sc, sh):
    ms = jnp.mean(x * x, axis=-1, keepdims=True)
    return (x * lax.rsqrt(ms + NORM_EPS) * g) * (1.0 + sc) + sh


def _inproj_body(x_ref, sc_ref, sh_ref, g_ref, w_ref, cos_ref, sin_ref,
                 qa_ref, ka_ref, va_ref, us_ref, qr_ref, kr_ref, vr_ref, gr_ref):
    hb = _modulated_norm(x_ref[...], g_ref[...], sc_ref[...], sh_ref[...]).astype(BF16)

    def proj(lo, width):
        return jnp.dot(hb, w_ref[:, lo:lo + width], preferred_element_type=F32)

    scale = HEAD_DIM ** -0.5
    a = ATTN_WIDTH
    qa_ref[...] = (proj(0, a) * scale).astype(BF16)
    ka_ref[...] = proj(a, a).astype(BF16)
    va_ref[...] = proj(2 * a, a).astype(BF16)
    o = 3 * a
    us_ref[...] = proj(o, SSM_WIDTH)
    o += SSM_WIDTH

    cos = cos_ref[...]
    sin = sin_ref[...]
    lane = lax.broadcasted_iota(jnp.int32, cos.shape, 1)
    first_half = (lane & (HEAD_DIM // 2)) == 0

    def rotary(z):
        partner = jnp.where(first_half,
                            pltpu.roll(z, RET_WIDTH - HEAD_DIM // 2, 1),
                            pltpu.roll(z, HEAD_DIM // 2, 1))
        return z * cos + partner * sin

    r = RET_WIDTH
    qr_ref[...] = rotary(proj(o, r)).astype(BF16)
    kr_ref[...] = (rotary(proj(o + r, r)) * scale).astype(BF16)
    vr_ref[...] = proj(o + 2 * r, r).astype(BF16)
    gr_ref[...] = proj(o + 3 * r, r)


def _inproj(x, sc, sh, g, w_bf16, cos_t, sin_t, ts):
    b, s, d = x.shape
    nj = s // ts
    tok = lambda width: pl.BlockSpec((None, ts, width), lambda bi, j: (bi, j, 0))
    per_b = pl.BlockSpec((None, 1, d), lambda bi, j: (bi, 0, 0))
    rope = pl.BlockSpec((ts, RET_WIDTH), lambda bi, j: (j, 0))
    sds = lambda width, dt: jax.ShapeDtypeStruct((b, s, width), dt)
    return pl.pallas_call(
        _inproj_body,
        grid=(b, nj),
        in_specs=[tok(d), per_b, per_b,
                  pl.BlockSpec((1, d), lambda bi, j: (0, 0)),
                  pl.BlockSpec(w_bf16.shape, lambda bi, j: (0, 0)),
                  rope, rope],
        out_specs=[tok(ATTN_WIDTH), tok(ATTN_WIDTH), tok(ATTN_WIDTH),
                   pl.BlockSpec((ts, SSM_WIDTH), lambda bi, j: (j, bi)),
                   tok(RET_WIDTH), tok(RET_WIDTH), tok(RET_WIDTH), tok(RET_WIDTH)],
        out_shape=[sds(ATTN_WIDTH, BF16), sds(ATTN_WIDTH, BF16), sds(ATTN_WIDTH, BF16),
                   jax.ShapeDtypeStruct((s, b * SSM_WIDTH), F32),
                   sds(RET_WIDTH, BF16), sds(RET_WIDTH, BF16), sds(RET_WIDTH, BF16), sds(RET_WIDTH, F32)],
        compiler_params=_cparams("parallel", "arbitrary"),
        name="inproj",
    )(x, sc, sh, g, w_bf16, cos_t, sin_t)


def _attn_body(q_ref, k0_ref, k1_ref, k2_ref, v0_ref, v1_ref, v2_ref, bias_ref, o_ref):
    j = pl.program_id(1)
    tq = q_ref.shape[0]
    col = lax.broadcasted_iota(jnp.int32, (tq, 3 * tq), 1)
    in_seq = col >= (2 - j) * tq
    lane = lax.broadcasted_iota(jnp.int32, (1, LANES), 1)
    low = lane < HEAD_DIM
    for hp in range(ATTN_HEADS // 2):
        sl = slice(hp * LANES, (hp + 1) * LANES)
        q2 = q_ref[:, sl]
        k2 = jnp.concatenate([k0_ref[:, sl], k1_ref[:, sl], k2_ref[:, sl]], axis=0)
        v2 = jnp.concatenate([v0_ref[:, sl], v1_ref[:, sl], v2_ref[:, sl]], axis=0)
        outs = []
        for sub in range(2):
            keep = low if sub == 0 else jnp.logical_not(low)
            kz = jnp.where(keep, k2, jnp.zeros_like(k2))
            s = lax.dot_general(q2, kz, (((1,), (1,)), ((), ())), preferred_element_type=F32)
            s = s + bias_ref[2 * hp + sub]
            s = jnp.where(in_seq, s, NEG_BIG)
            m = jnp.max(s, axis=-1, keepdims=True)
            p = jnp.exp(s - m)
            l = jnp.sum(p, axis=-1, keepdims=True)
            outs.append(jnp.dot(p.astype(BF16), v2, preferred_element_type=F32) / l)
        o_ref[:, sl] = jnp.where(low, outs[0], outs[1]).astype(BF16)


def _attn_bias_table(rel_bias, tq):
    r = np.arange(tq)[:, None]
    c = np.arange(3 * tq)[None, :]
    qc = r // CHUNK + 2 * tq // CHUNK
    kc = c // CHUNK
    band = (kc >= qc - LEFT_CHUNKS) & (kc <= qc)
    period = 4 * tq
    k = np.arange(period)
    diff = np.where(k < 3 * tq, k, k - period)
    idx = np.clip(diff - 2 * tq, -MAX_REL, CHUNK - 1) + MAX_REL
    vec = rel_bias.astype(F32)[:, idx]
    h = vec.shape[0]
    table = jnp.tile(vec, (1, tq))[:, :tq * (period - 1)].reshape(h, tq, period - 1)[:, :, :3 * tq]
    return jnp.where(jnp.asarray(band)[None], table, NEG_BIG)


def _attention(qa, ka, va, bias):
    b, s, w = qa.shape
    tq = ATTN_TQ
    qspec = pl.BlockSpec((None, tq, w), lambda bi, j: (bi, j, 0))
    kspec = lambda back: pl.BlockSpec((None, tq, w), lambda bi, j: (bi, jnp.maximum(j - back, 0), 0))
    return pl.pallas_call(
        _attn_body,
        grid=(b, s // tq),
        in_specs=[qspec, kspec(2), kspec(1), kspec(0), kspec(2), kspec(1), kspec(0),
                  pl.BlockSpec(bias.shape, lambda bi, j: (0, 0, 0))],
        out_specs=qspec,
        out_shape=jax.ShapeDtypeStruct((b, s, w), BF16),
        compiler_params=_cparams("parallel", "arbitrary"),
        name="band_attn",
    )(qa, ka, ka, ka, va, va, va, bias)


def _gelu_tanh(x):
    return 0.5 * x * (1.0 + jnp.tanh(math.sqrt(2.0 / math.pi) * (x + 0.044715 * (x * x * x))))


def _ssm_body(u_ref, bm_ref, cm_ref, ar_ref, ai_ref, d_ref, wg_ref, bg_ref, y_ref,
              st_ref, bu_ref, ut_ref, yt_ref):
    nb = st_ref.shape[1]
    ts = u_ref.shape[0]
    nc = SSM_COMPLEX
    w = SSM_WIDTH
    w_ct = w // LANES

    @pl.when(pl.program_id(0) == 0)
    def _():
        st_ref[...] = jnp.zeros_like(st_ref)

    for bi in range(nb):
        for ct in range(w_ct):
            lo = bi * w + ct * LANES
            ut_ref.at[ct][pl.ds(bi, ts, stride=nb), :] = u_ref[:, lo:lo + LANES]
    u_tb = jnp.concatenate([ut_ref[ct] for ct in range(w_ct)], axis=1)
    n_ct = 2 * nc // LANES
    bu = jnp.dot(u_tb.astype(BF16), bm_ref[...], preferred_element_type=F32)
    for ct in range(n_ct):
        bu_ref[ct] = bu[:, ct * LANES:(ct + 1) * LANES]

    tiles_per_block = 4
    half = nc // LANES
    for t0 in range(0, half, tiles_per_block):
        tiles = range(t0, t0 + tiles_per_block)
        ar = [jnp.broadcast_to(ar_ref[:, ct * LANES:(ct + 1) * LANES], (nb, LANES)) for ct in tiles]
        ai = [jnp.broadcast_to(ai_ref[:, ct * LANES:(ct + 1) * LANES], (nb, LANES)) for ct in tiles]

        def step(t, carry):
            rows = pl.ds(pl.multiple_of(t * nb, nb), nb)
            out = []
            for k, ct in enumerate(tiles):
                xr, xi = carry[2 * k], carry[2 * k + 1]
                nr = ar[k] * xr - ai[k] * xi + bu_ref[ct, rows, :]
                ni = ar[k] * xi + ai[k] * xr + bu_ref[half + ct, rows, :]
                bu_ref[ct, rows, :] = nr
                bu_ref[half + ct, rows, :] = ni
                out += [nr, ni]
            return tuple(out)

        init = []
        for ct in tiles:
            init += [st_ref[0, :, ct * LANES:(ct + 1) * LANES], st_ref[1, :, ct * LANES:(ct + 1) * LANES]]
        fin = lax.fori_loop(0, ts, step, tuple(init), unroll=4)
        for k, ct in enumerate(tiles):
            st_ref[0, :, ct * LANES:(ct + 1) * LANES] = fin[2 * k]
            st_ref[1, :, ct * LANES:(ct + 1) * LANES] = fin[2 * k + 1]

    states = jnp.concatenate([bu_ref[ct] for ct in range(n_ct)], axis=1).astype(BF16)
    u_tb = jnp.concatenate([ut_ref[ct] for ct in range(w_ct)], axis=1)
    y = _gelu_tanh(jnp.dot(states, cm_ref[...], preferred_element_type=F32) + d_ref[...] * u_tb)
    z = jnp.dot(y.astype(BF16), wg_ref[...], preferred_element_type=F32) + bg_ref[...]
    y = y * _sigmoid(z)
    for ct in range(w_ct):
        yt_ref[ct] = y[:, ct * LANES:(ct + 1) * LANES]
    for bi in range(nb):
        for ct in range(w_ct):
            lo = bi * w + ct * LANES
            y_ref[:, lo:lo + LANES] = yt_ref.at[ct][pl.ds(bi, ts, stride=nb), :].astype(BF16)


def _ssm_params(a_re, a_im, log_dt, b_re, b_im, c_re, c_im):
    g, p, cg = b_re.shape
    lam = lax.complex(a_re.astype(F32), a_im.astype(F32))
    dt = jnp.exp(log_dt.astype(F32))[:, None]
    a_bar = jnp.exp(lam * dt)
    b_bar = ((a_bar - 1.0) / lam)[:, :, None] * lax.complex(b_re.astype(F32), b_im.astype(F32))
    eye = jnp.eye(g, dtype=F32)
    b_blk = lambda m: jnp.einsum('gpc,gh->gchp', m, eye).reshape(g * cg, g * p)
    bm = jnp.concatenate([b_blk(jnp.real(b_bar)), b_blk(jnp.imag(b_bar))], axis=1)
    c_blk = lambda m: jnp.einsum('gcp,gh->gphc', m.astype(F32), eye).reshape(g * p, g * cg)
    cm = jnp.concatenate([c_blk(c_re), -c_blk(c_im)], axis=0)
    ar = jnp.real(a_bar).reshape(1, g * p)
    ai = jnp.imag(a_bar).reshape(1, g * p)
    return bm.astype(BF16), cm.astype(BF16), ar, ai


def _ssm(us, nb, bm, cm, ar, ai, d_skip, w_glu, b_glu):
    s_len, wide = us.shape
    w = SSM_WIDTH
    ts = SSM_TS
    full = lambda a: pl.BlockSpec(a.shape, lambda i: (0,) * a.ndim)
    d2 = d_skip.reshape(1, w).astype(F32)
    bg2 = b_glu.reshape(1, w).astype(F32)
    wg = w_glu.astype(BF16)
    return pl.pallas_call(
        _ssm_body,
        grid=(s_len // ts,),
        in_specs=[pl.BlockSpec((ts, wide), lambda i: (i, 0)),
                  full(bm), full(cm), full(ar), full(ai), full(d2), full(wg), full(bg2)],
        out_specs=pl.BlockSpec((ts, wide), lambda i: (i, 0)),
        out_shape=jax.ShapeDtypeStruct((s_len, wide), BF16),
        scratch_shapes=[pltpu.VMEM((2, nb, SSM_COMPLEX), F32),
                        pltpu.VMEM((2 * SSM_COMPLEX // LANES, ts * nb, LANES), F32),
                        pltpu.VMEM((w // LANES, ts * nb, LANES), F32),
                        pltpu.VMEM((w // LANES, ts * nb, LANES), F32)],
        compiler_params=_cparams("arbitrary"),
        name="s5",
    )(us, bm, cm, ar, ai, d2, wg, bg2)


def _ret_body(q_ref, k_ref, v_ref, g_ref, gn_ref, dec_ref, xi_ref, zeta_ref, cd_ref, avg_ref, o_ref, st_ref):
    @pl.when(pl.program_id(1) == 0)
    def _():
        st_ref[...] = jnp.zeros_like(st_ref)

    lane = lax.broadcasted_iota(jnp.int32, (1, LANES), 1)
    low = lane < HEAD_DIM
    avg = avg_ref[...]

    def head_mean(t):
        hi = t.astype(BF16)
        lo = (t - hi.astype(F32)).astype(BF16)
        return jnp.dot(hi, avg, preferred_element_type=F32) + jnp.dot(lo, avg, preferred_element_type=F32)

    for hp in range(RET_HEADS // 2):
        sl = slice(hp * LANES, (hp + 1) * LANES)
        q2 = q_ref[:, sl]
        k2 = k_ref[:, sl]
        v2 = v_ref[:, sl]
        parts = []
        for sub in range(2):
            keep = low if sub == 0 else jnp.logical_not(low)
            k_one = jnp.where(keep, k2, jnp.zeros_like(k2))
            a = lax.dot_general(q2, k_one, (((1,), (1,)), ((), ())), preferred_element_type=F32)
            a = a * dec_ref[2 * hp + sub]
            parts.append(jnp.dot(a.astype(BF16), v2, preferred_element_type=F32))
        inner = jnp.where(low, parts[0], parts[1])
        st = st_ref[hp]
        qx = (q2.astype(F32) * xi_ref[hp]).astype(BF16)
        y = inner + jnp.dot(qx, st.astype(BF16), preferred_element_type=F32)
        kz = (k2.astype(F32) * zeta_ref[hp]).astype(BF16)
        kv = lax.dot_general(kz, v2, (((0,), (0,)), ((), ())), preferred_element_type=F32)
        blocks = cd_ref[hp]
        st_ref[hp] = blocks * st + jnp.where(blocks > 0.0, kv, 0.0)
        yc = y - head_mean(y)
        var = head_mean(yc * yc)
        yn = yc * lax.rsqrt(var + NORM_EPS) * gn_ref[:, sl]
        g = g_ref[:, sl]
        o_ref[:, sl] = (g * _sigmoid(g) * yn).astype(BF16)


def _ret_tables(t_len):
    n_pairs = RET_HEADS // 2
    log_gamma = jnp.log1p(-jnp.exp2(-5.0 - jnp.arange(RET_HEADS, dtype=F32)))
    t = jnp.arange(t_len, dtype=F32)
    diff = t[:, None] - t[None, :]
    dec = jnp.where(diff >= 0, jnp.exp(log_gamma[:, None, None] * jnp.maximum(diff, 0.0)), 0.0)
    per_lane = lambda m: jnp.repeat(m.reshape(n_pairs, 2, -1), HEAD_DIM, axis=1).transpose(0, 2, 1)
    xi = per_lane(jnp.exp(log_gamma[:, None] * (t + 1.0)))
    zeta = per_lane(jnp.exp(log_gamma[:, None] * (t_len - 1 - t)))
    lane_head = jnp.arange(LANES) // HEAD_DIM
    same = (lane_head[:, None] == lane_head[None, :]).astype(F32)
    cd_lane = jnp.repeat(jnp.exp(log_gamma * t_len).reshape(n_pairs, 2), HEAD_DIM, axis=1)
    cd = cd_lane[:, :, None] * same[None]
    avg = (same / HEAD_DIM).astype(BF16)
    return dec, xi, zeta, cd, avg


def _retention(qr, kr, vr, gr, gn_g, tables):
    b, s, w = qr.shape
    t_len = RET_T
    dec, xi, zeta, cd, avg = tables
    tok = pl.BlockSpec((None, t_len, w), lambda bi, j: (bi, j, 0))
    full = lambda a: pl.BlockSpec(a.shape, lambda bi, j: (0,) * a.ndim)
    gn2 = gn_g.reshape(1, w).astype(F32)
    return pl.pallas_call(
        _ret_body,
        grid=(b, s // t_len),
        in_specs=[tok, tok, tok, tok, full(gn2), full(dec), full(xi), full(zeta), full(cd), full(avg)],
        out_specs=tok,
        out_shape=jax.ShapeDtypeStruct((b, s, w), BF16),
        scratch_shapes=[pltpu.VMEM((RET_HEADS // 2, LANES, LANES), F32)],
        compiler_params=_cparams("parallel", "arbitrary"),
        name="retention",
    )(qr, kr, vr, gr, gn2, dec, xi, zeta, cd, avg)


def _store_row_tiled(ref, val):
    rows = val.shape[0]
    for t in range(ROW_TILES):
        ref[pl.ds(t, rows, stride=ROW_TILES), :] = val[:, t * LANES:(t + 1) * LANES]


def _load_row_tiled(ref, first, rows):
    return jnp.concatenate(
        [ref[pl.ds(first * ROW_TILES + t, rows, stride=ROW_TILES), :] for t in range(ROW_TILES)], axis=1)


def _outproj_body(x_ref, ya_ref, ys_ref, yr_ref, w_ref, g1_ref, sc_ref, sh_ref, ng_ref, wr_ref, br_ref,
                  xo_ref, h2_ref, route_ref):
    a = ATTN_WIDTH
    mixed = jnp.dot(ya_ref[...], w_ref[0:a, :], preferred_element_type=F32)
    mixed += jnp.dot(ys_ref[...], w_ref[a:a + SSM_WIDTH, :], preferred_element_type=F32)
    mixed += jnp.dot(yr_ref[...], w_ref[a + SSM_WIDTH:, :], preferred_element_type=F32)
    x = x_ref[...] + g1_ref[...] * mixed
    xo_ref[...] = x
    h2 = _modulated_norm(x, ng_ref[...], sc_ref[...], sh_ref[...])
    _store_row_tiled(h2_ref, h2)

    h_hi = h2.astype(BF16)
    h_lo = (h2 - h_hi.astype(F32)).astype(BF16)
    logits = (jnp.dot(h_hi, wr_ref[0], preferred_element_type=F32)
              + (jnp.dot(h_hi, wr_ref[1], preferred_element_type=F32)
                 + jnp.dot(h_lo, wr_ref[0], preferred_element_type=F32))
              + br_ref[...])
    lane = lax.broadcasted_iota(jnp.int32, logits.shape, 1).astype(F32)
    big = float(LANES)

    def first_argmax(vals):
        m = jnp.max(vals, axis=-1, keepdims=True)
        return m, jnp.min(jnp.where(vals == m, lane, big), axis=-1, keepdims=True)

    gl = jnp.where(lane < N_GROUPS, logits, -jnp.inf)
    gmax, gidx = first_argmax(gl)
    group_w = 1.0 / jnp.sum(jnp.exp(gl - gmax), axis=-1, keepdims=True)
    e_lo = N_GROUPS + gidx * EXPERTS_PER_GROUP
    el = jnp.where((lane >= e_lo) & (lane < e_lo + EXPERTS_PER_GROUP), logits, -jnp.inf)
    v1, i1 = first_argmax(el)
    v2, i2 = first_argmax(jnp.where(lane == i1, -jnp.inf, el))
    t2 = jnp.exp(v2 - v1)
    w1 = group_w / (1.0 + t2)
    w2 = group_w * t2 / (1.0 + t2)
    route = jnp.where(lane == 0, i1 - N_GROUPS,
                      jnp.where(lane == 1, i2 - N_GROUPS,
                                jnp.where(lane == 2, w1, jnp.where(lane == 3, w2, 0.0))))
    route_ref[...] = route


def _outproj_router(x, ya, ys_tm, yr, w_out_bf16, g1, sc2, sh2, ng, w_route, b_route, ts):
    b, s, d = x.shape
    tok = lambda width: pl.BlockSpec((None, ts, width), lambda bi, j: (bi, j, 0))
    per_b = pl.BlockSpec((None, 1, d), lambda bi, j: (bi, 0, 0))
    full = lambda a: pl.BlockSpec(a.shape, lambda bi, j: (0,) * a.ndim)
    nj = s // ts
    return pl.pallas_call(
        _outproj_body,
        grid=(b, nj),
        in_specs=[tok(d), tok(ATTN_WIDTH),
                  pl.BlockSpec((ts, SSM_WIDTH), lambda bi, j: (j, bi)),
                  tok(RET_WIDTH), full(w_out_bf16), per_b, per_b, per_b, full(ng),
                  full(w_route), full(b_route)],
        out_specs=[tok(d),
                   pl.BlockSpec((ts * ROW_TILES, LANES), lambda bi, j: (bi * nj + j, 0)),
                   pl.BlockSpec((None, ts, LANES), lambda bi, j: (bi, j, 0))],
        out_shape=[jax.ShapeDtypeStruct((b, s, d), F32),
                   jax.ShapeDtypeStruct((b * s * ROW_TILES, LANES), F32),
                   jax.ShapeDtypeStruct((b, s, LANES), F32)],
        compiler_params=_cparams("parallel", "arbitrary"),
        name="outproj_router",
    )(x, ya, ys_tm, yr, w_out_bf16, g1, sc2, sh2, ng, w_route, b_route)


def _slot_plan(expert_ids, n_slots):
    tm = MOE_TM
    e = expert_ids.reshape(-1)
    onehot = (e[:, None] == jnp.arange(N_EXPERTS, dtype=jnp.int32)[None, :]).astype(jnp.int32)
    csum = jnp.cumsum(onehot, axis=0)
    rank = jnp.sum(onehot * (csum - 1), axis=1)
    counts = csum[-1]
    padded = ((counts + tm - 1) // tm) * tm
    ends = jnp.cumsum(padded)
    starts = ends - padded
    dest = (jnp.sum(onehot * starts[None, :], axis=1) + rank).astype(jnp.int32)
    n_tiles = n_slots // tm
    tile_row = jnp.arange(n_tiles, dtype=jnp.int32) * tm
    tile_expert = jnp.minimum(jnp.sum((tile_row[:, None] >= ends[None, :]).astype(jnp.int32), axis=1),
                              N_EXPERTS - 1).astype(jnp.int32)
    used_tiles = (ends[-1] // tm).astype(jnp.int32)
    group_last = jnp.where(padded > 0, ends - tm, n_slots)
    tail = jnp.where(tile_row >= ends[-1], tile_row, n_slots)
    fill = jnp.sort(jnp.concatenate([group_last, tail]).astype(jnp.int32))
    n_fill = jnp.sum(fill < n_slots).astype(jnp.int32)
    meta = jnp.stack([n_fill, used_tiles]).astype(jnp.int32)
    return dest, tile_expert, fill, meta


DMA_UNROLL = 8


def _dispatch_body(dest_ref, fill_ref, meta_ref, h2_ref, xs_hbm, zero_ref, sem):
    step = pl.program_id(0)
    tile_rows = MOE_TM * ROW_TILES
    tokens = h2_ref.shape[0] // ROW_TILES

    def fill_copy(i):
        row = pl.multiple_of(fill_ref[i] * ROW_TILES, tile_rows)
        return pltpu.make_async_copy(zero_ref, xs_hbm.at[pl.ds(row, tile_rows), :], sem)

    @pl.when(step == 0)
    def _():
        zero_ref[...] = jnp.zeros_like(zero_ref)
        n_fill = meta_ref[0]
        lax.fori_loop(0, n_fill, lambda i, c: (fill_copy(i).start(), c)[1], 0)
        lax.fori_loop(0, n_fill, lambda i, c: (fill_copy(i).wait(), c)[1], 0)

    base = step * (2 * tokens)

    def row_copy(i, choice):
        src = pl.multiple_of(i * ROW_TILES, ROW_TILES)
        dst = pl.multiple_of(dest_ref[base + 2 * i + choice] * ROW_TILES, ROW_TILES)
        return pltpu.make_async_copy(h2_ref.at[pl.ds(src, ROW_TILES), :],
                                     xs_hbm.at[pl.ds(dst, ROW_TILES), :], sem)

    def start(i, c):
        row_copy(i, 0).start(priority=0)
        row_copy(i, 1).start(priority=1)
        return c

    def wait(i, c):
        row_copy(i, 0).wait()
        row_copy(i, 1).wait()
        return c

    lax.fori_loop(0, tokens, start, 0, unroll=DMA_UNROLL)
    lax.fori_loop(0, tokens, wait, 0, unroll=DMA_UNROLL)


def _dispatch(h2_tiled, dest, fill, meta, n_slots):
    tokens = 512
    rows = tokens * ROW_TILES
    return pl.pallas_call(
        _dispatch_body,
        grid_spec=pltpu.PrefetchScalarGridSpec(
            num_scalar_prefetch=3,
            grid=(h2_tiled.shape[0] // rows,),
            in_specs=[pl.BlockSpec((rows, LANES), lambda i, d, f, m: (i, 0))],
            out_specs=pl.BlockSpec(memory_space=pl.ANY),
            scratch_shapes=[pltpu.VMEM((MOE_TM * ROW_TILES, LANES), F32), pltpu.SemaphoreType.DMA],
        ),
        out_shape=jax.ShapeDtypeStruct((n_slots * ROW_TILES, LANES), F32),
        compiler_params=pltpu.CompilerParams(dimension_semantics=("arbitrary",), has_side_effects=True,
                                             vmem_limit_bytes=VMEM_LIMIT),
        name="moe_dispatch",
    )(dest, fill, meta, h2_tiled)


def _experts_body(te_ref, meta_ref, xs_ref, wg_ref, wu_ref, wd_ref, ys_ref):
    tm = MOE_TM

    @pl.when(pl.program_id(0) < meta_ref[1])
    def _():
        x = _load_row_tiled(xs_ref, 0, tm).astype(BF16)
        hg = jnp.dot(x, wg_ref[...].astype(BF16), preferred_element_type=F32)
        hu = jnp.dot(x, wu_ref[...].astype(BF16), preferred_element_type=F32)
        act = (hg * _sigmoid(hg) * hu).astype(BF16)
        _store_row_tiled(ys_ref, jnp.dot(act, wd_ref[...].astype(BF16), preferred_element_type=F32))

    @pl.when(pl.program_id(0) >= meta_ref[1])
    def _():
        ys_ref[...] = jnp.zeros_like(ys_ref)


def _experts(xs, tile_expert, meta, w_gate, w_up, w_down, layer):
    n_tiles = tile_expert.shape[0]
    rows = MOE_TM * ROW_TILES
    d, f = w_gate.shape[2], w_gate.shape[3]
    return pl.pallas_call(
        _experts_body,
        grid_spec=pltpu.PrefetchScalarGridSpec(
            num_scalar_prefetch=2,
            grid=(n_tiles,),
            in_specs=[pl.BlockSpec((rows, LANES), lambda i, te, mt: (i, 0)),
                      pl.BlockSpec((None, None, d, f), lambda i, te, mt: (layer, te[i], 0, 0)),
                      pl.BlockSpec((None, None, d, f), lambda i, te, mt: (layer, te[i], 0, 0)),
                      pl.BlockSpec((None, None, f, d), lambda i, te, mt: (layer, te[i], 0, 0))],
            out_specs=pl.BlockSpec((rows, LANES), lambda i, te, mt: (i, 0)),
        ),
        out_shape=jax.ShapeDtypeStruct(xs.shape, F32),
        compiler_params=_cparams("arbitrary"),
        name="moe_experts",
    )(tile_expert, meta, xs, w_gate, w_up, w_down)


def _combine_body(dest_ref, x_ref, route_ref, g2_ref, fg_ref, ys_hbm, o_ref, buf_ref, sem, *, final_norm):
    tm = x_ref.shape[0]
    base = (pl.program_id(0) * pl.num_programs(1) + pl.program_id(1)) * tm

    def row_copy(i, choice):
        src = pl.multiple_of(dest_ref[(base + i) * 2 + choice] * ROW_TILES, ROW_TILES)
        dst = pl.multiple_of((choice * tm + i) * ROW_TILES, ROW_TILES)
        return pltpu.make_async_copy(ys_hbm.at[pl.ds(src, ROW_TILES), :],
                                     buf_ref.at[pl.ds(dst, ROW_TILES), :], sem)

    def start(i, c):
        row_copy(i, 0).start(priority=0)
        row_copy(i, 1).start(priority=1)
        return c

    def wait(i, c):
        row_copy(i, 0).wait()
        row_copy(i, 1).wait()
        return c

    lax.fori_loop(0, tm, start, 0, unroll=DMA_UNROLL)
    lax.fori_loop(0, tm, wait, 0, unroll=DMA_UNROLL)
    route = route_ref[...]
    y = route[:, 2:3] * _load_row_tiled(buf_ref, 0, tm) + route[:, 3:4] * _load_row_tiled(buf_ref, tm, tm)
    x = x_ref[...] + g2_ref[...] * y
    if final_norm:
        ms = jnp.mean(x * x, axis=-1, keepdims=True)
        x = x * lax.rsqrt(ms + NORM_EPS) * fg_ref[...]
    o_ref[...] = x


def _combine(x, route, g2, final_g, ys, dest, tm, final_norm):
    b, s, d = x.shape
    tok = lambda width: pl.BlockSpec((None, tm, width), lambda bi, j, dst: (bi, j, 0))
    return pl.pallas_call(
        functools.partial(_combine_body, final_norm=final_norm),
        grid_spec=pltpu.PrefetchScalarGridSpec(
            num_scalar_prefetch=1,
            grid=(b, s // tm),
            in_specs=[tok(d), tok(LANES),
                      pl.BlockSpec((None, 1, d), lambda bi, j, dst: (bi, 0, 0)),
                      pl.BlockSpec((1, d), lambda bi, j, dst: (0, 0)),
                      pl.BlockSpec(memory_space=pl.ANY)],
            out_specs=tok(d),
            scratch_shapes=[pltpu.VMEM((2 * tm * ROW_TILES, LANES), F32), pltpu.SemaphoreType.DMA],
        ),
        out_shape=jax.ShapeDtypeStruct((b, s, d), F32),
        compiler_params=_cparams("arbitrary", "arbitrary"),
        name="moe_combine",
    )(dest, x, route, g2, final_g, ys)


def _rope_tables(s):
    half = HEAD_DIM // 2
    inv_freq = ROPE_BASE ** (-jnp.arange(half, dtype=F32) / half)
    ang = jnp.arange(s, dtype=F32)[:, None] * inv_freq[None, :]
    cos = jnp.tile(jnp.cos(ang), (1, 2 * RET_HEADS))
    sin = jnp.sin(ang)
    sin = jnp.tile(jnp.concatenate([-sin, sin], axis=1), (1, RET_HEADS))
    return cos, sin


def _trunk(x, c, norm1_g, norm2_g, w_ada, b_ada, w_in, attn_rel_bias, ssm_a_re, ssm_a_im, ssm_log_dt,
           ssm_b_re, ssm_b_im, ssm_c_re, ssm_c_im, ssm_d, ssm_w_glu, ssm_b_glu, ret_gn_g, w_out,
           moe_w_group, moe_b_group, moe_w_expert, moe_b_expert, moe_w_gate, moe_w_up, moe_w_down, final_g,
           *, row_tile):
    b, s, d = x.shape
    n_layers = w_in.shape[0]
    n_slots = 2 * b * s + N_EXPERTS * MOE_TM
    mod = _adaln(c, w_ada, b_ada).reshape(n_layers, b, 6, 1, d)
    cos_t, sin_t = _rope_tables(s)
    ret_tables = _ret_tables(RET_T)
    w_in_b = w_in.astype(BF16)
    w_out_b = w_out.astype(BF16)
    n_route = N_GROUPS + N_EXPERTS
    w_route = jnp.pad(jnp.concatenate([moe_w_group, moe_w_expert], axis=-1).astype(F32),
                      ((0, 0), (0, 0), (0, LANES - n_route)))
    w_route_hi = w_route.astype(BF16)
    w_route = jnp.stack([w_route_hi, (w_route - w_route_hi.astype(F32)).astype(BF16)], axis=1)
    b_route = jnp.pad(jnp.concatenate([moe_b_group, moe_b_expert], axis=-1).astype(F32),
                      ((0, 0), (0, LANES - n_route))).reshape(n_layers, 1, LANES)
    fg = final_g.reshape(1, d).astype(F32)
    for i in range(n_layers):
        sh1, sc1, g1, sh2, sc2, g2 = (mod[i, :, k] for k in range(6))
        qa, ka, va, us, qr, kr, vr, gr = _inproj(x, sc1, sh1, norm1_g[i].reshape(1, d), w_in_b[i],
                                                 cos_t, sin_t, row_tile)
        y_a = _attention(qa, ka, va, _attn_bias_table(attn_rel_bias[i], ATTN_TQ))
        bm, cm, ar, ai = _ssm_params(ssm_a_re[i], ssm_a_im[i], ssm_log_dt[i], ssm_b_re[i], ssm_b_im[i],
                                     ssm_c_re[i], ssm_c_im[i])
        y_s = _ssm(us, b, bm, cm, ar, ai, ssm_d[i], ssm_w_glu[i], ssm_b_glu[i])
        y_r = _retention(qr, kr, vr, gr, ret_gn_g[i], ret_tables)
        x, h2, route = _outproj_router(x, y_a, y_s, y_r, w_out_b[i], g1, sc2, sh2,
                                       norm2_g[i].reshape(1, d), w_route[i], b_route[i], row_tile)
        expert_ids = route[:, :, 0:2].astype(jnp.int32).reshape(b * s, 2)
        dest, tile_expert, fill, meta = _slot_plan(expert_ids, n_slots)
        xs = _dispatch(h2, dest, fill, meta, n_slots)
        ys = _experts(xs, tile_expert, meta, moe_w_gate, moe_w_up, moe_w_down, i)
        x = _combine(x, route, g2, fg, ys, dest, 256, final_norm=(i == n_layers - 1))
    return x


def kernel(x, c, norm1_g, norm2_g, w_ada, b_ada, w_in, attn_rel_bias, ssm_a_re, ssm_a_im, ssm_log_dt, ssm_b_re, ssm_b_im, ssm_c_re, ssm_c_im, ssm_d, ssm_w_glu, ssm_b_glu, ret_gn_g, w_out, moe_w_group, moe_b_group, moe_w_expert, moe_b_expert, moe_w_gate, moe_w_up, moe_w_down, final_g):
    return _trunk(x, c, norm1_g, norm2_g, w_ada, b_ada, w_in, attn_rel_bias, ssm_a_re, ssm_a_im, ssm_log_dt,
                  ssm_b_re, ssm_b_im, ssm_c_re, ssm_c_im, ssm_d, ssm_w_glu, ssm_b_glu, ret_gn_g, w_out,
                  moe_w_group, moe_b_group, moe_w_expert, moe_b_expert, moe_w_gate, moe_w_up, moe_w_down,
                  final_g, row_tile=512)
```

```python
import functools
import math

import jax
import jax.numpy as jnp
import numpy as np
from jax import lax
from jax.experimental import pallas as pl
from jax.experimental.pallas import tpu as pltpu

F32 = jnp.float32
BF16 = jnp.bfloat16

D_MODEL = 1024
N_LAYERS = 4
CHUNK = 64
HEAD_DIM = 64
NORM_EPS = 1e-6

ATTN_HEADS = 8
ATTN_WIDTH = ATTN_HEADS * HEAD_DIM
LEFT_CHUNKS = 8
MAX_REL = 128
REL_TABLE = MAX_REL + CHUNK

SSM_WIDTH = 256
SSM_GROUP = 16
SSM_GROUPS = 16
SSM_STATE = 64
SSM_COMPLEX = SSM_GROUPS * SSM_STATE

RET_HEADS = 4
RET_WIDTH = RET_HEADS * HEAD_DIM
ROPE_BASE = 10000.0

N_GROUPS = 4
EXPERTS_PER_GROUP = 8
N_EXPERTS = N_GROUPS * EXPERTS_PER_GROUP
EXPERT_FF = 256

SUBLANES = 8
LANES = 128
VMEM_LIMIT = 56 * 1024 * 1024

ATTN_TQ = 256
RET_T = 256
SSM_TS = 64
MOE_TM = 512
ROW_TILES = D_MODEL // LANES
NEG_BIG = -1e30
LOG2_E = math.log2(math.e)


def _sigmoid(x):
    return 1.0 / (1.0 + jnp.exp(-x))


def _cparams(*sem):
    return pltpu.CompilerParams(dimension_semantics=sem, vmem_limit_bytes=VMEM_LIMIT)


def _adaln_body(c_ref, w_ref, b_ref, o_ref):
    c = c_ref[...]
    cond = c * _sigmoid(c)
    o_ref[...] = jnp.dot(cond, w_ref[...], preferred_element_type=F32,
                         precision=lax.Precision.HIGHEST) + b_ref[...]


def _adaln(c, w_ada, b_ada):
    n_l, d, d6 = w_ada.shape
    b = c.shape[0]
    tn = 1536
    return pl.pallas_call(
        _adaln_body,
        grid=(n_l, d6 // tn),
        in_specs=[pl.BlockSpec((b, d), lambda l, j: (0, 0)),
                  pl.BlockSpec((None, d, tn), lambda l, j: (l, 0, j)),
                  pl.BlockSpec((None, 1, tn), lambda l, j: (l, 0, j))],
        out_specs=pl.BlockSpec((None, b, tn), lambda l, j: (l, 0, j)),
        out_shape=jax.ShapeDtypeStruct((n_l, b, d6), F32),
        compiler_params=_cparams("arbitrary", "arbitrary"),
        name="adaln",
    )(c, w_ada, b_ada.reshape(n_l, 1, d6))


def _modulated_norm(x, g, sc, sh):
    ms = jnp.mean(x * x, axis=-1, keepdims=True)
    return (x * lax.rsqrt(ms + NORM_EPS) * g) * (1.0 + sc) + sh


def _inproj_body(x_ref, sc_ref, sh_ref, g_ref, w_ref, cos_ref, sin_ref,
                 qa_ref, ka_ref, va_ref, us_ref, qr_ref, kr_ref, vr_ref, gr_ref):
    hb = _modulated_norm(x_ref[...], g_ref[...], sc_ref[...], sh_ref[...]).astype(BF16)

    def proj(lo, width):
        return jnp.dot(hb, w_ref[:, lo:lo + width], preferred_element_type=F32)

    scale = HEAD_DIM ** -0.5
    a = ATTN_WIDTH
    qa_ref[...] = (proj(0, a) * (scale * LOG2_E)).astype(BF16)
    ka_ref[...] = proj(a, a).astype(BF16)
    va_ref[...] = proj(2 * a, a).astype(BF16)
    o = 3 * a
    us_ref[...] = proj(o, SSM_WIDTH)
    o += SSM_WIDTH

    cos = cos_ref[...]
    sin = sin_ref[...]
    lane = lax.broadcasted_iota(jnp.int32, cos.shape, 1)
    first_half = (lane & (HEAD_DIM // 2)) == 0

    def rotary(z):
        partner = jnp.where(first_half,
                            pltpu.roll(z, RET_WIDTH - HEAD_DIM // 2, 1),
                            pltpu.roll(z, HEAD_DIM // 2, 1))
        return z * cos + partner * sin

    r = RET_WIDTH
    qr_ref[...] = rotary(proj(o, r)).astype(BF16)
    kr_ref[...] = (rotary(proj(o + r, r)) * scale).astype(BF16)
    vr_ref[...] = proj(o + 2 * r, r).astype(BF16)
    gr_ref[...] = proj(o + 3 * r, r)


def _inproj(x, sc, sh, g, w_bf16, cos_t, sin_t, ts):
    b, s, d = x.shape
    nj = s // ts
    tok = lambda width: pl.BlockSpec((None, ts, width), lambda bi, j: (bi, j, 0))
    per_b = pl.BlockSpec((None, 1, d), lambda bi, j: (bi, 0, 0))
    rope = pl.BlockSpec((ts, RET_WIDTH), lambda bi, j: (j, 0))
    sds = lambda width, dt: jax.ShapeDtypeStruct((b, s, width), dt)
    return pl.pallas_call(
        _inproj_body,
        grid=(b, nj),
        in_specs=[tok(d), per_b, per_b,
                  pl.BlockSpec((1, d), lambda bi, j: (0, 0)),
                  pl.BlockSpec(w_bf16.shape, lambda bi, j: (0, 0)),
                  rope, rope],
        out_specs=[tok(ATTN_WIDTH), tok(ATTN_WIDTH), tok(ATTN_WIDTH),
                   pl.BlockSpec((ts, SSM_WIDTH), lambda bi, j: (j, bi)),
                   tok(RET_WIDTH), tok(RET_WIDTH), tok(RET_WIDTH), tok(RET_WIDTH)],
        out_shape=[sds(ATTN_WIDTH, BF16), sds(ATTN_WIDTH, BF16), sds(ATTN_WIDTH, BF16),
                   jax.ShapeDtypeStruct((s, b * SSM_WIDTH), F32),
                   sds(RET_WIDTH, BF16), sds(RET_WIDTH, BF16), sds(RET_WIDTH, BF16), sds(RET_WIDTH, F32)],
        compiler_params=_cparams("parallel", "arbitrary"),
        name="inproj",
    )(x, sc, sh, g, w_bf16, cos_t, sin_t)


def _attn_body(q_ref, k0_ref, k1_ref, k2_ref, v0_ref, v1_ref, v2_ref, bias_ref, o_ref):
    j = pl.program_id(1)
    tq = q_ref.shape[0]
    lane = lax.broadcasted_iota(jnp.int32, (1, LANES), 1)
    low = lane < HEAD_DIM
    ones_cols = jnp.ones((3 * tq, LANES), BF16)

    def heads(mask_start):
        if mask_start:
            col = lax.broadcasted_iota(jnp.int32, (tq, 3 * tq), 1)
            in_seq = col >= (2 - j) * tq
        for hp in range(ATTN_HEADS // 2):
            sl = slice(hp * LANES, (hp + 1) * LANES)
            q2 = q_ref[:, sl]
            k2 = jnp.concatenate([k0_ref[:, sl], k1_ref[:, sl], k2_ref[:, sl]], axis=0)
            v2 = jnp.concatenate([v0_ref[:, sl], v1_ref[:, sl], v2_ref[:, sl]], axis=0)
            v_aug = jnp.concatenate([v2, ones_cols], axis=1)
            outs = []
            for sub in range(2):
                keep = low if sub == 0 else jnp.logical_not(low)
                kz = jnp.where(keep, k2, jnp.zeros_like(k2))
                s = lax.dot_general(q2, kz, (((1,), (1,)), ((), ())), preferred_element_type=F32)
                s = s + bias_ref[2 * hp + sub]
                if mask_start:
                    s = jnp.where(in_seq, s, NEG_BIG)
                p = jnp.exp2(s - jnp.max(s, axis=-1, keepdims=True)).astype(BF16)
                r = jnp.dot(p, v_aug, preferred_element_type=F32)
                outs.append(r[:, :LANES] * (1.0 / r[:, LANES:LANES + 1]))
            o_ref[:, sl] = jnp.where(low, outs[0], outs[1]).astype(BF16)

    @pl.when(j < 2)
    def _():
        heads(True)

    @pl.when(j >= 2)
    def _():
        heads(False)


def _attn_bias_table(rel_bias, tq):
    r = np.arange(tq)[:, None]
    c = np.arange(3 * tq)[None, :]
    qc = r // CHUNK + 2 * tq // CHUNK
    kc = c // CHUNK
    band = (kc >= qc - LEFT_CHUNKS) & (kc <= qc)
    period = 4 * tq
    k = np.arange(period)
    diff = np.where(k < 3 * tq, k, k - period)
    idx = np.clip(diff - 2 * tq, -MAX_REL, CHUNK - 1) + MAX_REL
    vec = rel_bias.astype(F32)[:, idx]
    h = vec.shape[0]
    table = jnp.tile(vec, (1, tq))[:, :tq * (period - 1)].reshape(h, tq, period - 1)[:, :, :3 * tq]
    return jnp.where(jnp.asarray(band)[None], table * LOG2_E, NEG_BIG)


def _attention(qa, ka, va, bias):
    b, s, w = qa.shape
    tq = ATTN_TQ
    qspec = pl.BlockSpec((None, tq, w), lambda bi, j: (bi, j, 0))
    kspec = lambda back: pl.BlockSpec((None, tq, w), lambda bi, j: (bi, jnp.maximum(j - back, 0), 0))
    return pl.pallas_call(
        _attn_body,
        grid=(b, s // tq),
        in_specs=[qspec, kspec(2), kspec(1), kspec(0), kspec(2), kspec(1), kspec(0),
                  pl.BlockSpec(bias.shape, lambda bi, j: (0, 0, 0))],
        out_specs=qspec,
        out_shape=jax.ShapeDtypeStruct((b, s, w), BF16),
        compiler_params=_cparams("parallel", "arbitrary"),
        name="band_attn",
    )(qa, ka, ka, ka, va, va, va, bias)


def _gelu_tanh(x):
    return 0.5 * x * (1.0 + jnp.tanh(math.sqrt(2.0 / math.pi) * (x + 0.044715 * (x * x * x))))


def _ssm_body(u_ref, bm_ref, cm_ref, ar_ref, ai_ref, d_ref, wg_ref, bg_ref, y_ref,
              st_ref, bu_ref, ut_ref, yt_ref):
    nb = st_ref.shape[1]
    ts = u_ref.shape[0]
    nc = SSM_COMPLEX
    w = SSM_WIDTH
    w_ct = w // LANES

    @pl.when(pl.program_id(0) == 0)
    def _():
        st_ref[...] = jnp.zeros_like(st_ref)

    for bi in range(nb):
        for ct in range(w_ct):
            lo = bi * w + ct * LANES
            ut_ref.at[ct][pl.ds(bi, ts, stride=nb), :] = u_ref[:, lo:lo + LANES]
    u_tb = jnp.concatenate([ut_ref[ct] for ct in range(w_ct)], axis=1)
    n_ct = 2 * nc // LANES
    bu = jnp.dot(u_tb.astype(BF16), bm_ref[...], preferred_element_type=F32)
    for ct in range(n_ct):
        bu_ref[ct] = bu[:, ct * LANES:(ct + 1) * LANES]

    tiles_per_block = 4
    half = nc // LANES
    for t0 in range(0, half, tiles_per_block):
        tiles = range(t0, t0 + tiles_per_block)
        ar = [jnp.broadcast_to(ar_ref[:, ct * LANES:(ct + 1) * LANES], (nb, LANES)) for ct in tiles]
        ai = [jnp.broadcast_to(ai_ref[:, ct * LANES:(ct + 1) * LANES], (nb, LANES)) for ct in tiles]

        def step(t, carry):
            rows = pl.ds(pl.multiple_of(t * nb, nb), nb)
            out = []
            for k, ct in enumerate(tiles):
                xr, xi = carry[2 * k], carry[2 * k + 1]
                nr = ar[k] * xr - ai[k] * xi + bu_ref[ct, rows, :]
                ni = ar[k] * xi + ai[k] * xr + bu_ref[half + ct, rows, :]
                bu_ref[ct, rows, :] = nr
                bu_ref[half + ct, rows, :] = ni
                out += [nr, ni]
            return tuple(out)

        init = []
        for ct in tiles:
            init += [st_ref[0, :, ct * LANES:(ct + 1) * LANES], st_ref[1, :, ct * LANES:(ct + 1) * LANES]]
        fin = lax.fori_loop(0, ts, step, tuple(init), unroll=4)
        for k, ct in enumerate(tiles):
            st_ref[0, :, ct * LANES:(ct + 1) * LANES] = fin[2 * k]
            st_ref[1, :, ct * LANES:(ct + 1) * LANES] = fin[2 * k + 1]

    states = jnp.concatenate([bu_ref[ct] for ct in range(n_ct)], axis=1).astype(BF16)
    u_tb = jnp.concatenate([ut_ref[ct] for ct in range(w_ct)], axis=1)
    y = _gelu_tanh(jnp.dot(states, cm_ref[...], preferred_element_type=F32) + d_ref[...] * u_tb)
    z = jnp.dot(y.astype(BF16), wg_ref[...], preferred_element_type=F32) + bg_ref[...]
    y = y * _sigmoid(z)
    for ct in range(w_ct):
        yt_ref[ct] = y[:, ct * LANES:(ct + 1) * LANES]
    for bi in range(nb):
        for ct in range(w_ct):
            lo = bi * w + ct * LANES
            y_ref[:, lo:lo + LANES] = yt_ref.at[ct][pl.ds(bi, ts, stride=nb), :].astype(BF16)


def _ssm_params(a_re, a_im, log_dt, b_re, b_im, c_re, c_im):
    g, p, cg = b_re.shape
    lam = lax.complex(a_re.astype(F32), a_im.astype(F32))
    dt = jnp.exp(log_dt.astype(F32))[:, None]
    a_bar = jnp.exp(lam * dt)
    b_bar = ((a_bar - 1.0) / lam)[:, :, None] * lax.complex(b_re.astype(F32), b_im.astype(F32))
    eye = jnp.eye(g, dtype=F32)
    b_blk = lambda m: jnp.einsum('gpc,gh->gchp', m, eye).reshape(g * cg, g * p)
    bm = jnp.concatenate([b_blk(jnp.real(b_bar)), b_blk(jnp.imag(b_bar))], axis=1)
    c_blk = lambda m: jnp.einsum('gcp,gh->gphc', m.astype(F32), eye).reshape(g * p, g * cg)
    cm = jnp.concatenate([c_blk(c_re), -c_blk(c_im)], axis=0)
    ar = jnp.real(a_bar).reshape(1, g * p)
    ai = jnp.imag(a_bar).reshape(1, g * p)
    return bm.astype(BF16), cm.astype(BF16), ar, ai


def _ssm(us, nb, bm, cm, ar, ai, d_skip, w_glu, b_glu):
    s_len, wide = us.shape
    w = SSM_WIDTH
    ts = SSM_TS
    full = lambda a: pl.BlockSpec(a.shape, lambda i: (0,) * a.ndim)
    d2 = d_skip.reshape(1, w).astype(F32)
    bg2 = b_glu.reshape(1, w).astype(F32)
    wg = w_glu.astype(BF16)
    return pl.pallas_call(
        _ssm_body,
        grid=(s_len // ts,),
        in_specs=[pl.BlockSpec((ts, wide), lambda i: (i, 0)),
                  full(bm), full(cm), full(ar), full(ai), full(d2), full(wg), full(bg2)],
        out_specs=pl.BlockSpec((ts, wide), lambda i: (i, 0)),
        out_shape=jax.ShapeDtypeStruct((s_len, wide), BF16),
        scratch_shapes=[pltpu.VMEM((2, nb, SSM_COMPLEX), F32),
                        pltpu.VMEM((2 * SSM_COMPLEX // LANES, ts * nb, LANES), F32),
                        pltpu.VMEM((w // LANES, ts * nb, LANES), F32),
                        pltpu.VMEM((w // LANES, ts * nb, LANES), F32)],
        compiler_params=_cparams("arbitrary"),
        name="s5",
    )(us, bm, cm, ar, ai, d2, wg, bg2)


def _ret_body(q_ref, k_ref, v_ref, g_ref, gn_ref, dec_ref, xi_ref, zeta_ref, cd_ref, avg_ref, o_ref, st_ref):
    @pl.when(pl.program_id(1) == 0)
    def _():
        st_ref[...] = jnp.zeros_like(st_ref)

    lane = lax.broadcasted_iota(jnp.int32, (1, LANES), 1)
    low = lane < HEAD_DIM
    avg = avg_ref[...]

    def head_mean(t):
        hi = t.astype(BF16)
        lo = (t - hi.astype(F32)).astype(BF16)
        return jnp.dot(hi, avg, preferred_element_type=F32) + jnp.dot(lo, avg, preferred_element_type=F32)

    for hp in range(RET_HEADS // 2):
        sl = slice(hp * LANES, (hp + 1) * LANES)
        q2 = q_ref[:, sl]
        k2 = k_ref[:, sl]
        v2 = v_ref[:, sl]
        parts = []
        for sub in range(2):
            keep = low if sub == 0 else jnp.logical_not(low)
            k_one = jnp.where(keep, k2, jnp.zeros_like(k2))
            a = lax.dot_general(q2, k_one, (((1,), (1,)), ((), ())), preferred_element_type=F32)
            a = a * dec_ref[2 * hp + sub]
            parts.append(jnp.dot(a.astype(BF16), v2, preferred_element_type=F32))
        inner = jnp.where(low, parts[0], parts[1])
        st = st_ref[hp]
        qx = (q2.astype(F32) * xi_ref[hp]).astype(BF16)
        y = inner + jnp.dot(qx, st.astype(BF16), preferred_element_type=F32)
        kz = (k2.astype(F32) * zeta_ref[hp]).astype(BF16)
        kv = lax.dot_general(kz, v2, (((0,), (0,)), ((), ())), preferred_element_type=F32)
        blocks = cd_ref[hp]
        st_ref[hp] = blocks * st + jnp.where(blocks > 0.0, kv, 0.0)
        yc = y - head_mean(y)
        var = head_mean(yc * yc)
        yn = yc * lax.rsqrt(var + NORM_EPS) * gn_ref[:, sl]
        g = g_ref[:, sl]
        o_ref[:, sl] = (g * _sigmoid(g) * yn).astype(BF16)


def _ret_tables(t_len):
    n_pairs = RET_HEADS // 2
    log_gamma = jnp.log1p(-jnp.exp2(-5.0 - jnp.arange(RET_HEADS, dtype=F32)))
    t = jnp.arange(t_len, dtype=F32)
    diff = t[:, None] - t[None, :]
    dec = jnp.where(diff >= 0, jnp.exp(log_gamma[:, None, None] * jnp.maximum(diff, 0.0)), 0.0)
    per_lane = lambda m: jnp.repeat(m.reshape(n_pairs, 2, -1), HEAD_DIM, axis=1).transpose(0, 2, 1)
    xi = per_lane(jnp.exp(log_gamma[:, None] * (t + 1.0)))
    zeta = per_lane(jnp.exp(log_gamma[:, None] * (t_len - 1 - t)))
    lane_head = jnp.arange(LANES) // HEAD_DIM
    same = (lane_head[:, None] == lane_head[None, :]).astype(F32)
    cd_lane = jnp.repeat(jnp.exp(log_gamma * t_len).reshape(n_pairs, 2), HEAD_DIM, axis=1)
    cd = cd_lane[:, :, None] * same[None]
    avg = (same / HEAD_DIM).astype(BF16)
    return dec, xi, zeta, cd, avg


def _retention(qr, kr, vr, gr, gn_g, tables):
    b, s, w = qr.shape
    t_len = RET_T
    dec, xi, zeta, cd, avg = tables
    tok = pl.BlockSpec((None, t_len, w), lambda bi, j: (bi, j, 0))
    full = lambda a: pl.BlockSpec(a.shape, lambda bi, j: (0,) * a.ndim)
    gn2 = gn_g.reshape(1, w).astype(F32)
    return pl.pallas_call(
        _ret_body,
        grid=(b, s // t_len),
        in_specs=[tok, tok, tok, tok, full(gn2), full(dec), full(xi), full(zeta), full(cd), full(avg)],
        out_specs=tok,
        out_shape=jax.ShapeDtypeStruct((b, s, w), BF16),
        scratch_shapes=[pltpu.VMEM((RET_HEADS // 2, LANES, LANES), F32)],
        compiler_params=_cparams("parallel", "arbitrary"),
        name="retention",
    )(qr, kr, vr, gr, gn2, dec, xi, zeta, cd, avg)


def _store_row_tiled(ref, val):
    rows = val.shape[0]
    for t in range(ROW_TILES):
        ref[pl.ds(t, rows, stride=ROW_TILES), :] = val[:, t * LANES:(t + 1) * LANES]


def _load_row_tiled(ref, first, rows):
    return jnp.concatenate(
        [ref[pl.ds(first * ROW_TILES + t, rows, stride=ROW_TILES), :] for t in range(ROW_TILES)], axis=1)


def _outproj_body(x_ref, ya_ref, ys_ref, yr_ref, w_ref, g1_ref, sc_ref, sh_ref, ng_ref, wr_ref, br_ref,
                  xo_ref, h2_ref, route_ref):
    a = ATTN_WIDTH
    mixed = jnp.dot(ya_ref[...], w_ref[0:a, :], preferred_element_type=F32)
    mixed += jnp.dot(ys_ref[...], w_ref[a:a + SSM_WIDTH, :], preferred_element_type=F32)
    mixed += jnp.dot(yr_ref[...], w_ref[a + SSM_WIDTH:, :], preferred_element_type=F32)
    x = x_ref[...] + g1_ref[...] * mixed
    xo_ref[...] = x
    h2 = _modulated_norm(x, ng_ref[...], sc_ref[...], sh_ref[...])
    _store_row_tiled(h2_ref, h2)

    h_hi = h2.astype(BF16)
    h_lo = (h2 - h_hi.astype(F32)).astype(BF16)
    logits = (jnp.dot(h_hi, wr_ref[0], preferred_element_type=F32)
              + (jnp.dot(h_hi, wr_ref[1], preferred_element_type=F32)
                 + jnp.dot(h_lo, wr_ref[0], preferred_element_type=F32))
              + br_ref[...])
    lane = lax.broadcasted_iota(jnp.int32, logits.shape, 1).astype(F32)
    big = float(LANES)

    def first_argmax(vals):
        m = jnp.max(vals, axis=-1, keepdims=True)
        return m, jnp.min(jnp.where(vals == m, lane, big), axis=-1, keepdims=True)

    gl = jnp.where(lane < N_GROUPS, logits, -jnp.inf)
    gmax, gidx = first_argmax(gl)
    group_w = 1.0 / jnp.sum(jnp.exp(gl - gmax), axis=-1, keepdims=True)
    e_lo = N_GROUPS + gidx * EXPERTS_PER_GROUP
    el = jnp.where((lane >= e_lo) & (lane < e_lo + EXPERTS_PER_GROUP), logits, -jnp.inf)
    v1, i1 = first_argmax(el)
    v2, i2 = first_argmax(jnp.where(lane == i1, -jnp.inf, el))
    t2 = jnp.exp(v2 - v1)
    w1 = group_w / (1.0 + t2)
    w2 = group_w * t2 / (1.0 + t2)
    route = jnp.where(lane == 0, i1 - N_GROUPS,
                      jnp.where(lane == 1, i2 - N_GROUPS,
                                jnp.where(lane == 2, w1, jnp.where(lane == 3, w2, 0.0))))
    route_ref[...] = route


def _outproj_router(x, ya, ys_tm, yr, w_out_bf16, g1, sc2, sh2, ng, w_route, b_route, ts):
    b, s, d = x.shape
    tok = lambda width: pl.BlockSpec((None, ts, width), lambda bi, j: (bi, j, 0))
    per_b = pl.BlockSpec((None, 1, d), lambda bi, j: (bi, 0, 0))
    full = lambda a: pl.BlockSpec(a.shape, lambda bi, j: (0,) * a.ndim)
    nj = s // ts
    return pl.pallas_call(
        _outproj_body,
        grid=(b, nj),
        in_specs=[tok(d), tok(ATTN_WIDTH),
                  pl.BlockSpec((ts, SSM_WIDTH), lambda bi, j: (j, bi)),
                  tok(RET_WIDTH), full(w_out_bf16), per_b, per_b, per_b, full(ng),
                  full(w_route), full(b_route)],
        out_specs=[tok(d),
                   pl.BlockSpec((ts * ROW_TILES, LANES), lambda bi, j: (bi * nj + j, 0)),
                   pl.BlockSpec((None, ts, LANES), lambda bi, j: (bi, j, 0))],
        out_shape=[jax.ShapeDtypeStruct((b, s, d), F32),
                   jax.ShapeDtypeStruct((b * s * ROW_TILES, LANES), F32),
                   jax.ShapeDtypeStruct((b, s, LANES), F32)],
        compiler_params=_cparams("parallel", "arbitrary"),
        name="outproj_router",
    )(x, ya, ys_tm, yr, w_out_bf16, g1, sc2, sh2, ng, w_route, b_route)


def _slot_plan(expert_ids, n_slots):
    tm = MOE_TM
    e = expert_ids.reshape(-1)
    onehot = (e[:, None] == jnp.arange(N_EXPERTS, dtype=jnp.int32)[None, :]).astype(jnp.int32)
    csum = jnp.cumsum(onehot, axis=0)
    rank = jnp.sum(onehot * (csum - 1), axis=1)
    counts = csum[-1]
    padded = ((counts + tm - 1) // tm) * tm
    ends = jnp.cumsum(padded)
    starts = ends - padded
    dest = (jnp.sum(onehot * starts[None, :], axis=1) + rank).astype(jnp.int32)
    n_tiles = n_slots // tm
    tile_row = jnp.arange(n_tiles, dtype=jnp.int32) * tm
    tile_expert = jnp.minimum(jnp.sum((tile_row[:, None] >= ends[None, :]).astype(jnp.int32), axis=1),
                              N_EXPERTS - 1).astype(jnp.int32)
    used_tiles = (ends[-1] // tm).astype(jnp.int32)
    group_last = jnp.where(padded > 0, ends - tm, n_slots)
    tail = jnp.where(tile_row >= ends[-1], tile_row, n_slots)
    fill = jnp.sort(jnp.concatenate([group_last, tail]).astype(jnp.int32))
    n_fill = jnp.sum(fill < n_slots).astype(jnp.int32)
    meta = jnp.stack([n_fill, used_tiles]).astype(jnp.int32)
    return dest, tile_expert, fill, meta


DMA_UNROLL = 8


def _dispatch_body(dest_ref, fill_ref, meta_ref, h2_ref, xs_hbm, zero_ref, sem):
    step = pl.program_id(0)
    tile_rows = MOE_TM * ROW_TILES
    tokens = h2_ref.shape[0] // ROW_TILES

    def fill_copy(i):
        row = pl.multiple_of(fill_ref[i] * ROW_TILES, tile_rows)
        return pltpu.make_async_copy(zero_ref, xs_hbm.at[pl.ds(row, tile_rows), :], sem)

    @pl.when(step == 0)
    def _():
        zero_ref[...] = jnp.zeros_like(zero_ref)
        n_fill = meta_ref[0]
        lax.fori_loop(0, n_fill, lambda i, c: (fill_copy(i).start(), c)[1], 0)
        lax.fori_loop(0, n_fill, lambda i, c: (fill_copy(i).wait(), c)[1], 0)

    base = step * (2 * tokens)

    def row_copy(i, choice):
        src = pl.multiple_of(i * ROW_TILES, ROW_TILES)
        dst = pl.multiple_of(dest_ref[base + 2 * i + choice] * ROW_TILES, ROW_TILES)
        return pltpu.make_async_copy(h2_ref.at[pl.ds(src, ROW_TILES), :],
                                     xs_hbm.at[pl.ds(dst, ROW_TILES), :], sem)

    def start(i, c):
        row_copy(i, 0).start(priority=0)
        row_copy(i, 1).start(priority=1)
        return c

    def wait(i, c):
        row_copy(i, 0).wait()
        row_copy(i, 1).wait()
        return c

    lax.fori_loop(0, tokens, start, 0, unroll=DMA_UNROLL)
    lax.fori_loop(0, tokens, wait, 0, unroll=DMA_UNROLL)


def _dispatch(h2_tiled, dest, fill, meta, n_slots):
    tokens = 512
    rows = tokens * ROW_TILES
    return pl.pallas_call(
        _dispatch_body,
        grid_spec=pltpu.PrefetchScalarGridSpec(
            num_scalar_prefetch=3,
            grid=(h2_tiled.shape[0] // rows,),
            in_specs=[pl.BlockSpec((rows, LANES), lambda i, d, f, m: (i, 0))],
            out_specs=pl.BlockSpec(memory_space=pl.ANY),
            scratch_shapes=[pltpu.VMEM((MOE_TM * ROW_TILES, LANES), F32), pltpu.SemaphoreType.DMA],
        ),
        out_shape=jax.ShapeDtypeStruct((n_slots * ROW_TILES, LANES), F32),
        compiler_params=pltpu.CompilerParams(dimension_semantics=("arbitrary",), has_side_effects=True,
                                             vmem_limit_bytes=VMEM_LIMIT),
        name="moe_dispatch",
    )(dest, fill, meta, h2_tiled)


def _experts_body(te_ref, meta_ref, xs_ref, wg_ref, wu_ref, wd_ref, ys_ref):
    tm = MOE_TM

    @pl.when(pl.program_id(0) < meta_ref[1])
    def _():
        x = _load_row_tiled(xs_ref, 0, tm).astype(BF16)
        hg = jnp.dot(x, wg_ref[...].astype(BF16), preferred_element_type=F32)
        hu = jnp.dot(x, wu_ref[...].astype(BF16), preferred_element_type=F32)
        act = (hg * _sigmoid(hg) * hu).astype(BF16)
        _store_row_tiled(ys_ref, jnp.dot(act, wd_ref[...].astype(BF16), preferred_element_type=F32))

    @pl.when(pl.program_id(0) >= meta_ref[1])
    def _():
        ys_ref[...] = jnp.zeros_like(ys_ref)


def _experts(xs, tile_expert, meta, w_gate, w_up, w_down, layer):
    n_tiles = tile_expert.shape[0]
    rows = MOE_TM * ROW_TILES
    d, f = w_gate.shape[2], w_gate.shape[3]
    return pl.pallas_call(
        _experts_body,
        grid_spec=pltpu.PrefetchScalarGridSpec(
            num_scalar_prefetch=2,
            grid=(n_tiles,),
            in_specs=[pl.BlockSpec((rows, LANES), lambda i, te, mt: (i, 0)),
                      pl.BlockSpec((None, None, d, f), lambda i, te, mt: (layer, te[i], 0, 0)),
                      pl.BlockSpec((None, None, d, f), lambda i, te, mt: (layer, te[i], 0, 0)),
                      pl.BlockSpec((None, None, f, d), lambda i, te, mt: (layer, te[i], 0, 0))],
            out_specs=pl.BlockSpec((rows, LANES), lambda i, te, mt: (i, 0)),
        ),
        out_shape=jax.ShapeDtypeStruct(xs.shape, F32),
        compiler_params=_cparams("arbitrary"),
        name="moe_experts",
    )(tile_expert, meta, xs, w_gate, w_up, w_down)


def _combine_body(dest_ref, x_ref, route_ref, g2_ref, fg_ref, ys_hbm, o_ref, buf_ref, sem, *, final_norm):
    tm = x_ref.shape[0]
    base = (pl.program_id(0) * pl.num_programs(1) + pl.program_id(1)) * tm

    def row_copy(i, choice):
        src = pl.multiple_of(dest_ref[(base + i) * 2 + choice] * ROW_TILES, ROW_TILES)
        dst = pl.multiple_of((choice * tm + i) * ROW_TILES, ROW_TILES)
        return pltpu.make_async_copy(ys_hbm.at[pl.ds(src, ROW_TILES), :],
                                     buf_ref.at[pl.ds(dst, ROW_TILES), :], sem)

    def start(i, c):
        row_copy(i, 0).start(priority=0)
        row_copy(i, 1).start(priority=1)
        return c

    def wait(i, c):
        row_copy(i, 0).wait()
        row_copy(i, 1).wait()
        return c

    lax.fori_loop(0, tm, start, 0, unroll=DMA_UNROLL)
    lax.fori_loop(0, tm, wait, 0, unroll=DMA_UNROLL)
    route = route_ref[...]
    y = route[:, 2:3] * _load_row_tiled(buf_ref, 0, tm) + route[:, 3:4] * _load_row_tiled(buf_ref, tm, tm)
    x = x_ref[...] + g2_ref[...] * y
    if final_norm:
        ms = jnp.mean(x * x, axis=-1, keepdims=True)
        x = x * lax.rsqrt(ms + NORM_EPS) * fg_ref[...]
    o_ref[...] = x


def _combine(x, route, g2, final_g, ys, dest, tm, final_norm):
    b, s, d = x.shape
    tok = lambda width: pl.BlockSpec((None, tm, width), lambda bi, j, dst: (bi, j, 0))
    return pl.pallas_call(
        functools.partial(_combine_body, final_norm=final_norm),
        grid_spec=pltpu.PrefetchScalarGridSpec(
            num_scalar_prefetch=1,
            grid=(b, s // tm),
            in_specs=[tok(d), tok(LANES),
                      pl.BlockSpec((None, 1, d), lambda bi, j, dst: (bi, 0, 0)),
                      pl.BlockSpec((1, d), lambda bi, j, dst: (0, 0)),
                      pl.BlockSpec(memory_space=pl.ANY)],
            out_specs=tok(d),
            scratch_shapes=[pltpu.VMEM((2 * tm * ROW_TILES, LANES), F32), pltpu.SemaphoreType.DMA],
        ),
        out_shape=jax.ShapeDtypeStruct((b, s, d), F32),
        compiler_params=_cparams("arbitrary", "arbitrary"),
        name="moe_combine",
    )(dest, x, route, g2, final_g, ys)


def _rope_tables(s):
    half = HEAD_DIM // 2
    inv_freq = ROPE_BASE ** (-jnp.arange(half, dtype=F32) / half)
    ang = jnp.arange(s, dtype=F32)[:, None] * inv_freq[None, :]
    cos = jnp.tile(jnp.cos(ang), (1, 2 * RET_HEADS))
    sin = jnp.sin(ang)
    sin = jnp.tile(jnp.concatenate([-sin, sin], axis=1), (1, RET_HEADS))
    return cos, sin


def _trunk(x, c, norm1_g, norm2_g, w_ada, b_ada, w_in, attn_rel_bias, ssm_a_re, ssm_a_im, ssm_log_dt,
           ssm_b_re, ssm_b_im, ssm_c_re, ssm_c_im, ssm_d, ssm_w_glu, ssm_b_glu, ret_gn_g, w_out,
           moe_w_group, moe_b_group, moe_w_expert, moe_b_expert, moe_w_gate, moe_w_up, moe_w_down, final_g,
           *, row_tile):
    b, s, d = x.shape
    n_layers = w_in.shape[0]
    n_slots = 2 * b * s + N_EXPERTS * MOE_TM
    mod = _adaln(c, w_ada, b_ada).reshape(n_layers, b, 6, 1, d)
    cos_t, sin_t = _rope_tables(s)
    ret_tables = _ret_tables(RET_T)
    w_in_b = w_in.astype(BF16)
    w_out_b = w_out.astype(BF16)
    n_route = N_GROUPS + N_EXPERTS
    w_route = jnp.pad(jnp.concatenate([moe_w_group, moe_w_expert], axis=-1).astype(F32),
                      ((0, 0), (0, 0), (0, LANES - n_route)))
    w_route_hi = w_route.astype(BF16)
    w_route = jnp.stack([w_route_hi, (w_route - w_route_hi.astype(F32)).astype(BF16)], axis=1)
    b_route = jnp.pad(jnp.concatenate([moe_b_group, moe_b_expert], axis=-1).astype(F32),
                      ((0, 0), (0, LANES - n_route))).reshape(n_layers, 1, LANES)
    fg = final_g.reshape(1, d).astype(F32)
    for i in range(n_layers):
        sh1, sc1, g1, sh2, sc2, g2 = (mod[i, :, k] for k in range(6))
        qa, ka, va, us, qr, kr, vr, gr = _inproj(x, sc1, sh1, norm1_g[i].reshape(1, d), w_in_b[i],
                                                 cos_t, sin_t, row_tile)
        y_a = _attention(qa, ka, va, _attn_bias_table(attn_rel_bias[i], ATTN_TQ))
        bm, cm, ar, ai = _ssm_params(ssm_a_re[i], ssm_a_im[i], ssm_log_dt[i], ssm_b_re[i], ssm_b_im[i],
                                     ssm_c_re[i], ssm_c_im[i])
        y_s = _ssm(us, b, bm, cm, ar, ai, ssm_d[i], ssm_w_glu[i], ssm_b_glu[i])
        y_r = _retention(qr, kr, vr, gr, ret_gn_g[i], ret_tables)
        x, h2, route = _outproj_router(x, y_a, y_s, y_r, w_out_b[i], g1, sc2, sh2,
                                       norm2_g[i].reshape(1, d), w_route[i], b_route[i], row_tile)
        expert_ids = route[:, :, 0:2].astype(jnp.int32).reshape(b * s, 2)
        dest, tile_expert, fill, meta = _slot_plan(expert_ids, n_slots)
        xs = _dispatch(h2, dest, fill, meta, n_slots)
        ys = _experts(xs, tile_expert, meta, moe_w_gate, moe_w_up, moe_w_down, i)
        x = _combine(x, route, g2, fg, ys, dest, 256, final_norm=(i == n_layers - 1))
    return x


def kernel(x, c, norm1_g, norm2_g, w_ada, b_ada, w_in, attn_rel_bias, ssm_a_re, ssm_a_im, ssm_log_dt, ssm_b_re, ssm_b_im, ssm_c_re, ssm_c_im, ssm_d, ssm_w_glu, ssm_b_glu, ret_gn_g, w_out, moe_w_group, moe_b_group, moe_w_expert, moe_b_expert, moe_w_gate, moe_w_up, moe_w_down, final_g):
    return _trunk(x, c, norm1_g, norm2_g, w_ada, b_ada, w_in, attn_rel_bias, ssm_a_re, ssm_a_im, ssm_log_dt,
                  ssm_b_re, ssm_b_im, ssm_c_re, ssm_c_im, ssm_d, ssm_w_glu, ssm_b_glu, ret_gn_g, w_out,
                  moe_w_group, moe_b_group, moe_w_expert, moe_b_expert, moe_w_gate, moe_w_up, moe_w_down,
                  final_g, row_tile=512)
```

```python
import functools
import math

import jax
import jax.numpy as jnp
import numpy as np
from jax import lax
from jax.experimental import pallas as pl
from jax.experimental.pallas import tpu as pltpu

F32 = jnp.float32
BF16 = jnp.bfloat16

D_MODEL = 1024
N_LAYERS = 4
CHUNK = 64
HEAD_DIM = 64
NORM_EPS = 1e-6

ATTN_HEADS = 8
ATTN_WIDTH = ATTN_HEADS * HEAD_DIM
LEFT_CHUNKS = 8
MAX_REL = 128
REL_TABLE = MAX_REL + CHUNK

SSM_WIDTH = 256
SSM_GROUP = 16
SSM_GROUPS = 16
SSM_STATE = 64
SSM_COMPLEX = SSM_GROUPS * SSM_STATE

RET_HEADS = 4
RET_WIDTH = RET_HEADS * HEAD_DIM
ROPE_BASE = 10000.0

N_GROUPS = 4
EXPERTS_PER_GROUP = 8
N_EXPERTS = N_GROUPS * EXPERTS_PER_GROUP
EXPERT_FF = 256

SUBLANES = 8
LANES = 128
VMEM_LIMIT = 56 * 1024 * 1024

ATTN_TQ = 256
RET_T = 256
SSM_TS = 64
MOE_TM = 512
ROW_TILES = D_MODEL // LANES
NEG_BIG = -1e30
LOG2_E = math.log2(math.e)


def _sigmoid(x):
    return 1.0 / (1.0 + jnp.exp(-x))


def _cparams(*sem):
    return pltpu.CompilerParams(dimension_semantics=sem, vmem_limit_bytes=VMEM_LIMIT)


def _adaln_body(c_ref, w_ref, b_ref, o_ref):
    c = c_ref[...]
    cond = c * _sigmoid(c)
    o_ref[...] = jnp.dot(cond, w_ref[...], preferred_element_type=F32,
                         precision=lax.Precision.HIGHEST) + b_ref[...]


def _adaln(c, w_ada, b_ada):
    n_l, d, d6 = w_ada.shape
    b = c.shape[0]
    tn = 1536
    return pl.pallas_call(
        _adaln_body,
        grid=(n_l, d6 // tn),
        in_specs=[pl.BlockSpec((b, d), lambda l, j: (0, 0)),
                  pl.BlockSpec((None, d, tn), lambda l, j: (l, 0, j)),
                  pl.BlockSpec((None, 1, tn), lambda l, j: (l, 0, j))],
        out_specs=pl.BlockSpec((None, b, tn), lambda l, j: (l, 0, j)),
        out_shape=jax.ShapeDtypeStruct((n_l, b, d6), F32),
        compiler_params=_cparams("arbitrary", "arbitrary"),
        name="adaln",
    )(c, w_ada, b_ada.reshape(n_l, 1, d6))


def _modulated_norm(x, g, sc, sh):
    ms = jnp.mean(x * x, axis=-1, keepdims=True)
    return (x * lax.rsqrt(ms + NORM_EPS) * g) * (1.0 + sc) + sh


def _inproj_body(x_ref, *refs):
    _inproj_project(x_ref[...], *refs)


def _inproj_project(x, sc_ref, sh_ref, g_ref, w_ref, cos_ref, sin_ref,
                    qa_ref, ka_ref, va_ref, us_ref, qr_ref, kr_ref, vr_ref, gr_ref):
    hb = _modulated_norm(x, g_ref[...], sc_ref[...], sh_ref[...]).astype(BF16)

    def proj(lo, width):
        return jnp.dot(hb, w_ref[:, lo:lo + width], preferred_element_type=F32)

    scale = HEAD_DIM ** -0.5
    a = ATTN_WIDTH
    qa_ref[...] = (proj(0, a) * (scale * LOG2_E)).astype(BF16)
    ka_ref[...] = proj(a, a).astype(BF16)
    va_ref[...] = proj(2 * a, a).astype(BF16)
    o = 3 * a
    us_ref[...] = proj(o, SSM_WIDTH)
    o += SSM_WIDTH

    cos = cos_ref[...]
    sin = sin_ref[...]
    lane = lax.broadcasted_iota(jnp.int32, cos.shape, 1)
    first_half = (lane & (HEAD_DIM // 2)) == 0

    def rotary(z):
        partner = jnp.where(first_half,
                            pltpu.roll(z, RET_WIDTH - HEAD_DIM // 2, 1),
                            pltpu.roll(z, HEAD_DIM // 2, 1))
        return z * cos + partner * sin

    r = RET_WIDTH
    qr_ref[...] = rotary(proj(o, r)).astype(BF16)
    kr_ref[...] = (rotary(proj(o + r, r)) * scale).astype(BF16)
    vr_ref[...] = proj(o + 2 * r, r).astype(BF16)
    gr_ref[...] = proj(o + 3 * r, r)


def _inproj(x, sc, sh, g, w_bf16, cos_t, sin_t, ts):
    b, s, d = x.shape
    nj = s // ts
    tok = lambda width: pl.BlockSpec((None, ts, width), lambda bi, j: (bi, j, 0))
    per_b = pl.BlockSpec((None, 1, d), lambda bi, j: (bi, 0, 0))
    rope = pl.BlockSpec((ts, RET_WIDTH), lambda bi, j: (j, 0))
    sds = lambda width, dt: jax.ShapeDtypeStruct((b, s, width), dt)
    return pl.pallas_call(
        _inproj_body,
        grid=(b, nj),
        in_specs=[tok(d), per_b, per_b,
                  pl.BlockSpec((1, d), lambda bi, j: (0, 0)),
                  pl.BlockSpec(w_bf16.shape, lambda bi, j: (0, 0)),
                  rope, rope],
        out_specs=[tok(ATTN_WIDTH), tok(ATTN_WIDTH), tok(ATTN_WIDTH),
                   pl.BlockSpec((ts, SSM_WIDTH), lambda bi, j: (j, bi)),
                   tok(RET_WIDTH), tok(RET_WIDTH), tok(RET_WIDTH), tok(RET_WIDTH)],
        out_shape=[sds(ATTN_WIDTH, BF16), sds(ATTN_WIDTH, BF16), sds(ATTN_WIDTH, BF16),
                   jax.ShapeDtypeStruct((s, b * SSM_WIDTH), F32),
                   sds(RET_WIDTH, BF16), sds(RET_WIDTH, BF16), sds(RET_WIDTH, BF16), sds(RET_WIDTH, F32)],
        compiler_params=_cparams("parallel", "arbitrary"),
        name="inproj",
    )(x, sc, sh, g, w_bf16, cos_t, sin_t)


def _attn_body(q_ref, k0_ref, k1_ref, k2_ref, v0_ref, v1_ref, v2_ref, bias_ref, o_ref):
    j = pl.program_id(1)
    tq = q_ref.shape[0]
    lane = lax.broadcasted_iota(jnp.int32, (1, LANES), 1)
    low = lane < HEAD_DIM
    ones_cols = jnp.ones((3 * tq, LANES), BF16)

    def heads(mask_start):
        if mask_start:
            col = lax.broadcasted_iota(jnp.int32, (tq, 3 * tq), 1)
            in_seq = col >= (2 - j) * tq
        for hp in range(ATTN_HEADS // 2):
            sl = slice(hp * LANES, (hp + 1) * LANES)
            q2 = q_ref[:, sl]
            k2 = jnp.concatenate([k0_ref[:, sl], k1_ref[:, sl], k2_ref[:, sl]], axis=0)
            v2 = jnp.concatenate([v0_ref[:, sl], v1_ref[:, sl], v2_ref[:, sl]], axis=0)
            v_aug = jnp.concatenate([v2, ones_cols], axis=1)
            outs = []
            for sub in range(2):
                keep = low if sub == 0 else jnp.logical_not(low)
                kz = jnp.where(keep, k2, jnp.zeros_like(k2))
                s = lax.dot_general(q2, kz, (((1,), (1,)), ((), ())), preferred_element_type=F32)
                s = s + bias_ref[2 * hp + sub]
                if mask_start:
                    s = jnp.where(in_seq, s, NEG_BIG)
                p = jnp.exp2(s - jnp.max(s, axis=-1, keepdims=True)).astype(BF16)
                r = jnp.dot(p, v_aug, preferred_element_type=F32)
                outs.append(r[:, :LANES] * (1.0 / r[:, LANES:LANES + 1]))
            o_ref[:, sl] = jnp.where(low, outs[0], outs[1]).astype(BF16)

    @pl.when(j < 2)
    def _():
        heads(True)

    @pl.when(j >= 2)
    def _():
        heads(False)


def _attn_bias_table(rel_bias, tq):
    r = np.arange(tq)[:, None]
    c = np.arange(3 * tq)[None, :]
    qc = r // CHUNK + 2 * tq // CHUNK
    kc = c // CHUNK
    band = (kc >= qc - LEFT_CHUNKS) & (kc <= qc)
    period = 4 * tq
    k = np.arange(period)
    diff = np.where(k < 3 * tq, k, k - period)
    idx = np.clip(diff - 2 * tq, -MAX_REL, CHUNK - 1) + MAX_REL
    vec = rel_bias.astype(F32)[:, idx]
    h = vec.shape[0]
    table = jnp.tile(vec, (1, tq))[:, :tq * (period - 1)].reshape(h, tq, period - 1)[:, :, :3 * tq]
    return jnp.where(jnp.asarray(band)[None], table * LOG2_E, NEG_BIG)


def _attention(qa, ka, va, bias):
    b, s, w = qa.shape
    tq = ATTN_TQ
    qspec = pl.BlockSpec((None, tq, w), lambda bi, j: (bi, j, 0))
    kspec = lambda back: pl.BlockSpec((None, tq, w), lambda bi, j: (bi, jnp.maximum(j - back, 0), 0))
    return pl.pallas_call(
        _attn_body,
        grid=(b, s // tq),
        in_specs=[qspec, kspec(2), kspec(1), kspec(0), kspec(2), kspec(1), kspec(0),
                  pl.BlockSpec(bias.shape, lambda bi, j: (0, 0, 0))],
        out_specs=qspec,
        out_shape=jax.ShapeDtypeStruct((b, s, w), BF16),
        compiler_params=_cparams("parallel", "arbitrary"),
        name="band_attn",
    )(qa, ka, ka, ka, va, va, va, bias)


def _gelu_tanh(x):
    return 0.5 * x * (1.0 + jnp.tanh(math.sqrt(2.0 / math.pi) * (x + 0.044715 * (x * x * x))))


def _ssm_body(u_ref, bm_ref, cm_ref, ar_ref, ai_ref, d_ref, wg_ref, bg_ref, y_ref,
              st_ref, bu_ref, ut_ref, yt_ref):
    nb = st_ref.shape[1]
    ts = u_ref.shape[0]
    nc = SSM_COMPLEX
    w = SSM_WIDTH
    w_ct = w // LANES

    @pl.when(pl.program_id(0) == 0)
    def _():
        st_ref[...] = jnp.zeros_like(st_ref)

    for bi in range(nb):
        for ct in range(w_ct):
            lo = bi * w + ct * LANES
            ut_ref.at[ct][pl.ds(bi, ts, stride=nb), :] = u_ref[:, lo:lo + LANES]
    u_tb = jnp.concatenate([ut_ref[ct] for ct in range(w_ct)], axis=1)
    n_ct = 2 * nc // LANES
    bu = jnp.dot(u_tb.astype(BF16), bm_ref[...], preferred_element_type=F32)
    for ct in range(n_ct):
        bu_ref[ct] = bu[:, ct * LANES:(ct + 1) * LANES]

    tiles_per_block = 4
    half = nc // LANES
    for t0 in range(0, half, tiles_per_block):
        tiles = range(t0, t0 + tiles_per_block)
        ar = [jnp.broadcast_to(ar_ref[:, ct * LANES:(ct + 1) * LANES], (nb, LANES)) for ct in tiles]
        ai = [jnp.broadcast_to(ai_ref[:, ct * LANES:(ct + 1) * LANES], (nb, LANES)) for ct in tiles]

        def step(t, carry):
            rows = pl.ds(pl.multiple_of(t * nb, nb), nb)
            out = []
            for k, ct in enumerate(tiles):
                xr, xi = carry[2 * k], carry[2 * k + 1]
                nr = ar[k] * xr - ai[k] * xi + bu_ref[ct, rows, :]
                ni = ar[k] * xi + ai[k] * xr + bu_ref[half + ct, rows, :]
                bu_ref[ct, rows, :] = nr
                bu_ref[half + ct, rows, :] = ni
                out += [nr, ni]
            return tuple(out)

        init = []
        for ct in tiles:
            init += [st_ref[0, :, ct * LANES:(ct + 1) * LANES], st_ref[1, :, ct * LANES:(ct + 1) * LANES]]
        fin = lax.fori_loop(0, ts, step, tuple(init), unroll=4)
        for k, ct in enumerate(tiles):
            st_ref[0, :, ct * LANES:(ct + 1) * LANES] = fin[2 * k]
            st_ref[1, :, ct * LANES:(ct + 1) * LANES] = fin[2 * k + 1]

    states = jnp.concatenate([bu_ref[ct] for ct in range(n_ct)], axis=1).astype(BF16)
    u_tb = jnp.concatenate([ut_ref[ct] for ct in range(w_ct)], axis=1)
    y = _gelu_tanh(jnp.dot(states, cm_ref[...], preferred_element_type=F32) + d_ref[...] * u_tb)
    z = jnp.dot(y.astype(BF16), wg_ref[...], preferred_element_type=F32) + bg_ref[...]
    y = y * _sigmoid(z)
    for ct in range(w_ct):
        yt_ref[ct] = y[:, ct * LANES:(ct + 1) * LANES]
    for bi in range(nb):
        for ct in range(w_ct):
            lo = bi * w + ct * LANES
            y_ref[:, lo:lo + LANES] = yt_ref.at[ct][pl.ds(bi, ts, stride=nb), :].astype(BF16)


def _ssm_params(a_re, a_im, log_dt, b_re, b_im, c_re, c_im):
    g, p, cg = b_re.shape
    lam = lax.complex(a_re.astype(F32), a_im.astype(F32))
    dt = jnp.exp(log_dt.astype(F32))[:, None]
    a_bar = jnp.exp(lam * dt)
    b_bar = ((a_bar - 1.0) / lam)[:, :, None] * lax.complex(b_re.astype(F32), b_im.astype(F32))
    eye = jnp.eye(g, dtype=F32)
    b_blk = lambda m: jnp.einsum('gpc,gh->gchp', m, eye).reshape(g * cg, g * p)
    bm = jnp.concatenate([b_blk(jnp.real(b_bar)), b_blk(jnp.imag(b_bar))], axis=1)
    c_blk = lambda m: jnp.einsum('gcp,gh->gphc', m.astype(F32), eye).reshape(g * p, g * cg)
    cm = jnp.concatenate([c_blk(c_re), -c_blk(c_im)], axis=0)
    ar = jnp.real(a_bar).reshape(1, g * p)
    ai = jnp.imag(a_bar).reshape(1, g * p)
    return bm.astype(BF16), cm.astype(BF16), ar, ai


def _ssm(us, nb, bm, cm, ar, ai, d_skip, w_glu, b_glu):
    s_len, wide = us.shape
    w = SSM_WIDTH
    ts = SSM_TS
    full = lambda a: pl.BlockSpec(a.shape, lambda i: (0,) * a.ndim)
    d2 = d_skip.reshape(1, w).astype(F32)
    bg2 = b_glu.reshape(1, w).astype(F32)
    wg = w_glu.astype(BF16)
    return pl.pallas_call(
        _ssm_body,
        grid=(s_len // ts,),
        in_specs=[pl.BlockSpec((ts, wide), lambda i: (i, 0)),
                  full(bm), full(cm), full(ar), full(ai), full(d2), full(wg), full(bg2)],
        out_specs=pl.BlockSpec((ts, wide), lambda i: (i, 0)),
        out_shape=jax.ShapeDtypeStruct((s_len, wide), BF16),
        scratch_shapes=[pltpu.VMEM((2, nb, SSM_COMPLEX), F32),
                        pltpu.VMEM((2 * SSM_COMPLEX // LANES, ts * nb, LANES), F32),
                        pltpu.VMEM((w // LANES, ts * nb, LANES), F32),
                        pltpu.VMEM((w // LANES, ts * nb, LANES), F32)],
        compiler_params=_cparams("arbitrary"),
        name="s5",
    )(us, bm, cm, ar, ai, d2, wg, bg2)


def _ret_body(q_ref, k_ref, v_ref, g_ref, gn_ref, dec_ref, xi_ref, zeta_ref, cd_ref, avg_ref, o_ref, st_ref):
    @pl.when(pl.program_id(1) == 0)
    def _():
        st_ref[...] = jnp.zeros_like(st_ref)

    lane = lax.broadcasted_iota(jnp.int32, (1, LANES), 1)
    low = lane < HEAD_DIM
    avg = avg_ref[...]

    def head_mean(t):
        hi = t.astype(BF16)
        lo = (t - hi.astype(F32)).astype(BF16)
        return jnp.dot(hi, avg, preferred_element_type=F32) + jnp.dot(lo, avg, preferred_element_type=F32)

    for hp in range(RET_HEADS // 2):
        sl = slice(hp * LANES, (hp + 1) * LANES)
        q2 = q_ref[:, sl]
        k2 = k_ref[:, sl]
        v2 = v_ref[:, sl]
        parts = []
        for sub in range(2):
            keep = low if sub == 0 else jnp.logical_not(low)
            k_one = jnp.where(keep, k2, jnp.zeros_like(k2))
            a = lax.dot_general(q2, k_one, (((1,), (1,)), ((), ())), preferred_element_type=F32)
            a = a * dec_ref[2 * hp + sub]
            parts.append(jnp.dot(a.astype(BF16), v2, preferred_element_type=F32))
        inner = jnp.where(low, parts[0], parts[1])
        st = st_ref[hp]
        qx = (q2.astype(F32) * xi_ref[hp]).astype(BF16)
        y = inner + jnp.dot(qx, st.astype(BF16), preferred_element_type=F32)
        kz = (k2.astype(F32) * zeta_ref[hp]).astype(BF16)
        kv = lax.dot_general(kz, v2, (((0,), (0,)), ((), ())), preferred_element_type=F32)
        blocks = cd_ref[hp]
        st_ref[hp] = blocks * st + jnp.where(blocks > 0.0, kv, 0.0)
        yc = y - head_mean(y)
        var = head_mean(yc * yc)
        yn = yc * lax.rsqrt(var + NORM_EPS) * gn_ref[:, sl]
        g = g_ref[:, sl]
        o_ref[:, sl] = (g * _sigmoid(g) * yn).astype(BF16)


def _ret_tables(t_len):
    n_pairs = RET_HEADS // 2
    log_gamma = jnp.log1p(-jnp.exp2(-5.0 - jnp.arange(RET_HEADS, dtype=F32)))
    t = jnp.arange(t_len, dtype=F32)
    diff = t[:, None] - t[None, :]
    dec = jnp.where(diff >= 0, jnp.exp(log_gamma[:, None, None] * jnp.maximum(diff, 0.0)), 0.0)
    per_lane = lambda m: jnp.repeat(m.reshape(n_pairs, 2, -1), HEAD_DIM, axis=1).transpose(0, 2, 1)
    xi = per_lane(jnp.exp(log_gamma[:, None] * (t + 1.0)))
    zeta = per_lane(jnp.exp(log_gamma[:, None] * (t_len - 1 - t)))
    lane_head = jnp.arange(LANES) // HEAD_DIM
    same = (lane_head[:, None] == lane_head[None, :]).astype(F32)
    cd_lane = jnp.repeat(jnp.exp(log_gamma * t_len).reshape(n_pairs, 2), HEAD_DIM, axis=1)
    cd = cd_lane[:, :, None] * same[None]
    avg = (same / HEAD_DIM).astype(BF16)
    return dec, xi, zeta, cd, avg


def _retention(qr, kr, vr, gr, gn_g, tables):
    b, s, w = qr.shape
    t_len = RET_T
    dec, xi, zeta, cd, avg = tables
    tok = pl.BlockSpec((None, t_len, w), lambda bi, j: (bi, j, 0))
    full = lambda a: pl.BlockSpec(a.shape, lambda bi, j: (0,) * a.ndim)
    gn2 = gn_g.reshape(1, w).astype(F32)
    return pl.pallas_call(
        _ret_body,
        grid=(b, s // t_len),
        in_specs=[tok, tok, tok, tok, full(gn2), full(dec), full(xi), full(zeta), full(cd), full(avg)],
        out_specs=tok,
        out_shape=jax.ShapeDtypeStruct((b, s, w), BF16),
        scratch_shapes=[pltpu.VMEM((RET_HEADS // 2, LANES, LANES), F32)],
        compiler_params=_cparams("parallel", "arbitrary"),
        name="retention",
    )(qr, kr, vr, gr, gn2, dec, xi, zeta, cd, avg)


def _store_row_tiled(ref, val):
    rows = val.shape[0]
    for t in range(ROW_TILES):
        ref[pl.ds(t, rows, stride=ROW_TILES), :] = val[:, t * LANES:(t + 1) * LANES]


def _load_row_tiled(ref, first, rows):
    return jnp.concatenate(
        [ref[pl.ds(first * ROW_TILES + t, rows, stride=ROW_TILES), :] for t in range(ROW_TILES)], axis=1)


def _outproj_body(x_ref, ya_ref, ys_ref, yr_ref, w_ref, g1_ref, sc_ref, sh_ref, ng_ref, wr_ref, br_ref,
                  xo_ref, h2_ref, route_ref):
    a = ATTN_WIDTH
    mixed = jnp.dot(ya_ref[...], w_ref[0:a, :], preferred_element_type=F32)
    mixed += jnp.dot(ys_ref[...], w_ref[a:a + SSM_WIDTH, :], preferred_element_type=F32)
    mixed += jnp.dot(yr_ref[...], w_ref[a + SSM_WIDTH:, :], preferred_element_type=F32)
    x = x_ref[...] + g1_ref[...] * mixed
    xo_ref[...] = x
    h2 = _modulated_norm(x, ng_ref[...], sc_ref[...], sh_ref[...])
    _store_row_tiled(h2_ref, h2)

    h_hi = h2.astype(BF16)
    h_lo = (h2 - h_hi.astype(F32)).astype(BF16)
    logits = (jnp.dot(h_hi, wr_ref[0], preferred_element_type=F32)
              + (jnp.dot(h_hi, wr_ref[1], preferred_element_type=F32)
                 + jnp.dot(h_lo, wr_ref[0], preferred_element_type=F32))
              + br_ref[...])
    lane = lax.broadcasted_iota(jnp.int32, logits.shape, 1).astype(F32)
    big = float(LANES)

    def first_argmax(vals):
        m = jnp.max(vals, axis=-1, keepdims=True)
        return m, jnp.min(jnp.where(vals == m, lane, big), axis=-1, keepdims=True)

    gl = jnp.where(lane < N_GROUPS, logits, -jnp.inf)
    gmax, gidx = first_argmax(gl)
    group_w = 1.0 / jnp.sum(jnp.exp(gl - gmax), axis=-1, keepdims=True)
    e_lo = N_GROUPS + gidx * EXPERTS_PER_GROUP
    el = jnp.where((lane >= e_lo) & (lane < e_lo + EXPERTS_PER_GROUP), logits, -jnp.inf)
    v1, i1 = first_argmax(el)
    v2, i2 = first_argmax(jnp.where(lane == i1, -jnp.inf, el))
    t2 = jnp.exp(v2 - v1)
    w1 = group_w / (1.0 + t2)
    w2 = group_w * t2 / (1.0 + t2)
    route = jnp.where(lane == 0, i1 - N_GROUPS,
                      jnp.where(lane == 1, i2 - N_GROUPS,
                                jnp.where(lane == 2, w1, jnp.where(lane == 3, w2, 0.0))))
    route_ref[...] = route


def _outproj_router(x, ya, ys_tm, yr, w_out_bf16, g1, sc2, sh2, ng, w_route, b_route, ts):
    b, s, d = x.shape
    tok = lambda width: pl.BlockSpec((None, ts, width), lambda bi, j: (bi, j, 0))
    per_b = pl.BlockSpec((None, 1, d), lambda bi, j: (bi, 0, 0))
    full = lambda a: pl.BlockSpec(a.shape, lambda bi, j: (0,) * a.ndim)
    nj = s // ts
    return pl.pallas_call(
        _outproj_body,
        grid=(b, nj),
        in_specs=[tok(d), tok(ATTN_WIDTH),
                  pl.BlockSpec((ts, SSM_WIDTH), lambda bi, j: (j, bi)),
                  tok(RET_WIDTH), full(w_out_bf16), per_b, per_b, per_b, full(ng),
                  full(w_route), full(b_route)],
        out_specs=[tok(d),
                   pl.BlockSpec((ts * ROW_TILES, LANES), lambda bi, j: (bi * nj + j, 0)),
                   pl.BlockSpec((None, ts, LANES), lambda bi, j: (bi, j, 0))],
        out_shape=[jax.ShapeDtypeStruct((b, s, d), F32),
                   jax.ShapeDtypeStruct((b * s * ROW_TILES, LANES), F32),
                   jax.ShapeDtypeStruct((b, s, LANES), F32)],
        compiler_params=_cparams("parallel", "arbitrary"),
        name="outproj_router",
    )(x, ya, ys_tm, yr, w_out_bf16, g1, sc2, sh2, ng, w_route, b_route)


def _slot_plan(expert_ids, n_slots):
    tm = MOE_TM
    e = expert_ids.reshape(-1)
    onehot = (e[:, None] == jnp.arange(N_EXPERTS, dtype=jnp.int32)[None, :]).astype(jnp.int32)
    csum = jnp.cumsum(onehot, axis=0)
    rank = jnp.sum(onehot * (csum - 1), axis=1)
    counts = csum[-1]
    padded = ((counts + tm - 1) // tm) * tm
    ends = jnp.cumsum(padded)
    starts = ends - padded
    dest = (jnp.sum(onehot * starts[None, :], axis=1) + rank).astype(jnp.int32)
    n_tiles = n_slots // tm
    tile_row = jnp.arange(n_tiles, dtype=jnp.int32) * tm
    tile_expert = jnp.minimum(jnp.sum((tile_row[:, None] >= ends[None, :]).astype(jnp.int32), axis=1),
                              N_EXPERTS - 1).astype(jnp.int32)
    used_tiles = (ends[-1] // tm).astype(jnp.int32)
    group_last = jnp.where(padded > 0, ends - tm, n_slots)
    tail = jnp.where(tile_row >= ends[-1], tile_row, n_slots)
    fill = jnp.sort(jnp.concatenate([group_last, tail]).astype(jnp.int32))
    n_fill = jnp.sum(fill < n_slots).astype(jnp.int32)
    meta = jnp.stack([n_fill, used_tiles]).astype(jnp.int32)
    return dest, tile_expert, fill, meta


DMA_UNROLL = 8


def _dispatch_body(dest_ref, fill_ref, meta_ref, h2_ref, xs_hbm, zero_ref, sem):
    step = pl.program_id(0)
    tile_rows = MOE_TM * ROW_TILES
    tokens = h2_ref.shape[0] // ROW_TILES

    def fill_copy(i):
        row = pl.multiple_of(fill_ref[i] * ROW_TILES, tile_rows)
        return pltpu.make_async_copy(zero_ref, xs_hbm.at[pl.ds(row, tile_rows), :], sem)

    @pl.when(step == 0)
    def _():
        zero_ref[...] = jnp.zeros_like(zero_ref)
        n_fill = meta_ref[0]
        lax.fori_loop(0, n_fill, lambda i, c: (fill_copy(i).start(), c)[1], 0)
        lax.fori_loop(0, n_fill, lambda i, c: (fill_copy(i).wait(), c)[1], 0)

    base = step * (2 * tokens)

    def row_copy(i, choice):
        src = pl.multiple_of(i * ROW_TILES, ROW_TILES)
        dst = pl.multiple_of(dest_ref[base + 2 * i + choice] * ROW_TILES, ROW_TILES)
        return pltpu.make_async_copy(h2_ref.at[pl.ds(src, ROW_TILES), :],
                                     xs_hbm.at[pl.ds(dst, ROW_TILES), :], sem)

    def start(i, c):
        row_copy(i, 0).start(priority=0)
        row_copy(i, 1).start(priority=1)
        return c

    def wait(i, c):
        row_copy(i, 0).wait()
        row_copy(i, 1).wait()
        return c

    lax.fori_loop(0, tokens, start, 0, unroll=DMA_UNROLL)
    lax.fori_loop(0, tokens, wait, 0, unroll=DMA_UNROLL)


def _dispatch(h2_tiled, dest, fill, meta, n_slots):
    tokens = 512
    rows = tokens * ROW_TILES
    return pl.pallas_call(
        _dispatch_body,
        grid_spec=pltpu.PrefetchScalarGridSpec(
            num_scalar_prefetch=3,
            grid=(h2_tiled.shape[0] // rows,),
            in_specs=[pl.BlockSpec((rows, LANES), lambda i, d, f, m: (i, 0))],
            out_specs=pl.BlockSpec(memory_space=pl.ANY),
            scratch_shapes=[pltpu.VMEM((MOE_TM * ROW_TILES, LANES), F32), pltpu.SemaphoreType.DMA],
        ),
        out_shape=jax.ShapeDtypeStruct((n_slots * ROW_TILES, LANES), F32),
        compiler_params=pltpu.CompilerParams(dimension_semantics=("arbitrary",), has_side_effects=True,
                                             vmem_limit_bytes=VMEM_LIMIT),
        name="moe_dispatch",
    )(dest, fill, meta, h2_tiled)


def _experts_body(te_ref, meta_ref, xs_ref, wg_ref, wu_ref, wd_ref, ys_ref):
    tm = MOE_TM

    @pl.when(pl.program_id(0) < meta_ref[1])
    def _():
        x = _load_row_tiled(xs_ref, 0, tm).astype(BF16)
        hg = jnp.dot(x, wg_ref[...].astype(BF16), preferred_element_type=F32)
        hu = jnp.dot(x, wu_ref[...].astype(BF16), preferred_element_type=F32)
        act = (hg * _sigmoid(hg) * hu).astype(BF16)
        _store_row_tiled(ys_ref, jnp.dot(act, wd_ref[...].astype(BF16), preferred_element_type=F32))

    @pl.when(pl.program_id(0) >= meta_ref[1])
    def _():
        ys_ref[...] = jnp.zeros_like(ys_ref)


def _experts(xs, tile_expert, meta, w_gate, w_up, w_down, layer):
    n_tiles = tile_expert.shape[0]
    rows = MOE_TM * ROW_TILES
    d, f = w_gate.shape[2], w_gate.shape[3]
    return pl.pallas_call(
        _experts_body,
        grid_spec=pltpu.PrefetchScalarGridSpec(
            num_scalar_prefetch=2,
            grid=(n_tiles,),
            in_specs=[pl.BlockSpec((rows, LANES), lambda i, te, mt: (i, 0)),
                      pl.BlockSpec((None, None, d, f), lambda i, te, mt: (layer, te[i], 0, 0)),
                      pl.BlockSpec((None, None, d, f), lambda i, te, mt: (layer, te[i], 0, 0)),
                      pl.BlockSpec((None, None, f, d), lambda i, te, mt: (layer, te[i], 0, 0))],
            out_specs=pl.BlockSpec((rows, LANES), lambda i, te, mt: (i, 0)),
        ),
        out_shape=jax.ShapeDtypeStruct(xs.shape, F32),
        compiler_params=_cparams("arbitrary"),
        name="moe_experts",
    )(tile_expert, meta, xs, w_gate, w_up, w_down)


def _combine_body(dest_ref, x_ref, route_ref, g2_ref, fg_ref, ys_hbm, o_ref, buf_ref, sem, *, final_norm):
    tm = x_ref.shape[0]
    base = (pl.program_id(0) * pl.num_programs(1) + pl.program_id(1)) * tm

    def row_copy(i, choice):
        src = pl.multiple_of(dest_ref[(base + i) * 2 + choice] * ROW_TILES, ROW_TILES)
        dst = pl.multiple_of((choice * tm + i) * ROW_TILES, ROW_TILES)
        return pltpu.make_async_copy(ys_hbm.at[pl.ds(src, ROW_TILES), :],
                                     buf_ref.at[pl.ds(dst, ROW_TILES), :], sem)

    def start(i, c):
        row_copy(i, 0).start(priority=0)
        row_copy(i, 1).start(priority=1)
        return c

    def wait(i, c):
        row_copy(i, 0).wait()
        row_copy(i, 1).wait()
        return c

    lax.fori_loop(0, tm, start, 0, unroll=DMA_UNROLL)
    lax.fori_loop(0, tm, wait, 0, unroll=DMA_UNROLL)
    route = route_ref[...]
    y = route[:, 2:3] * _load_row_tiled(buf_ref, 0, tm) + route[:, 3:4] * _load_row_tiled(buf_ref, tm, tm)
    x = x_ref[...] + g2_ref[...] * y
    if final_norm:
        ms = jnp.mean(x * x, axis=-1, keepdims=True)
        x = x * lax.rsqrt(ms + NORM_EPS) * fg_ref[...]
    o_ref[...] = x


def _combine(x, route, g2, final_g, ys, dest, tm, final_norm):
    b, s, d = x.shape
    tok = lambda width: pl.BlockSpec((None, tm, width), lambda bi, j, dst: (bi, j, 0))
    return pl.pallas_call(
        functools.partial(_combine_body, final_norm=final_norm),
        grid_spec=pltpu.PrefetchScalarGridSpec(
            num_scalar_prefetch=1,
            grid=(b, s // tm),
            in_specs=[tok(d), tok(LANES),
                      pl.BlockSpec((None, 1, d), lambda bi, j, dst: (bi, 0, 0)),
                      pl.BlockSpec((1, d), lambda bi, j, dst: (0, 0)),
                      pl.BlockSpec(memory_space=pl.ANY)],
            out_specs=tok(d),
            scratch_shapes=[pltpu.VMEM((2 * tm * ROW_TILES, LANES), F32), pltpu.SemaphoreType.DMA],
        ),
        out_shape=jax.ShapeDtypeStruct((b, s, d), F32),
        compiler_params=_cparams("arbitrary", "arbitrary"),
        name="moe_combine",
    )(dest, x, route, g2, final_g, ys)


def _combine_inproj_body(dest_ref, x_ref, route_ref, g2_ref, ys_hbm, sc_ref, sh_ref, g_ref, w_ref, cos_ref,
                         sin_ref, xo_ref, qa_ref, ka_ref, va_ref, us_ref, qr_ref, kr_ref, vr_ref, gr_ref,
                         buf_ref, sems):
    tm = x_ref.shape[0]
    n_tiles = pl.num_programs(0) * pl.num_programs(1)
    tile = pl.program_id(0) * pl.num_programs(1) + pl.program_id(1)
    slot = tile % 2

    def row_copy(t, s, i, choice):
        src = pl.multiple_of(dest_ref[(t * tm + i) * 2 + choice] * ROW_TILES, ROW_TILES)
        dst = pl.multiple_of(((2 * s + choice) * tm + i) * ROW_TILES, ROW_TILES)
        return pltpu.make_async_copy(ys_hbm.at[pl.ds(src, ROW_TILES), :],
                                     buf_ref.at[pl.ds(dst, ROW_TILES), :], sems.at[s])

    def start_tile(t, s):
        def start(i, c):
            row_copy(t, s, i, 0).start(priority=0)
            row_copy(t, s, i, 1).start(priority=1)
            return c
        lax.fori_loop(0, tm, start, 0, unroll=DMA_UNROLL)

    @pl.when(tile == 0)
    def _():
        start_tile(tile, slot)

    @pl.when(tile + 1 < n_tiles)
    def _():
        start_tile(tile + 1, 1 - slot)

    def wait(i, c):
        row_copy(tile, slot, i, 0).wait()
        row_copy(tile, slot, i, 1).wait()
        return c

    lax.fori_loop(0, tm, wait, 0, unroll=DMA_UNROLL)
    route = route_ref[...]
    first = 2 * slot * tm
    y = (route[:, 2:3] * _load_row_tiled(buf_ref, first, tm)
         + route[:, 3:4] * _load_row_tiled(buf_ref, first + tm, tm))
    x = x_ref[...] + g2_ref[...] * y
    xo_ref[...] = x
    _inproj_project(x, sc_ref, sh_ref, g_ref, w_ref, cos_ref, sin_ref,
                    qa_ref, ka_ref, va_ref, us_ref, qr_ref, kr_ref, vr_ref, gr_ref)


def _combine_inproj(x, route, g2, ys, dest, sc, sh, g, w_bf16, cos_t, sin_t, ts):
    b, s, d = x.shape
    tok = lambda width: pl.BlockSpec((None, ts, width), lambda bi, j, dst: (bi, j, 0))
    per_b = pl.BlockSpec((None, 1, d), lambda bi, j, dst: (bi, 0, 0))
    rope = pl.BlockSpec((ts, RET_WIDTH), lambda bi, j, dst: (j, 0))
    sds = lambda width, dt: jax.ShapeDtypeStruct((b, s, width), dt)
    return pl.pallas_call(
        _combine_inproj_body,
        grid_spec=pltpu.PrefetchScalarGridSpec(
            num_scalar_prefetch=1,
            grid=(b, s // ts),
            in_specs=[tok(d), tok(LANES), per_b, pl.BlockSpec(memory_space=pl.ANY), per_b, per_b,
                      pl.BlockSpec((1, d), lambda bi, j, dst: (0, 0)),
                      pl.BlockSpec(w_bf16.shape, lambda bi, j, dst: (0, 0)),
                      rope, rope],
            out_specs=[tok(d), tok(ATTN_WIDTH), tok(ATTN_WIDTH), tok(ATTN_WIDTH),
                       pl.BlockSpec((ts, SSM_WIDTH), lambda bi, j, dst: (j, bi)),
                       tok(RET_WIDTH), tok(RET_WIDTH), tok(RET_WIDTH), tok(RET_WIDTH)],
            scratch_shapes=[pltpu.VMEM((2 * 2 * ts * ROW_TILES, LANES), F32), pltpu.SemaphoreType.DMA((2,))],
        ),
        out_shape=[sds(d, F32), sds(ATTN_WIDTH, BF16), sds(ATTN_WIDTH, BF16), sds(ATTN_WIDTH, BF16),
                   jax.ShapeDtypeStruct((s, b * SSM_WIDTH), F32),
                   sds(RET_WIDTH, BF16), sds(RET_WIDTH, BF16), sds(RET_WIDTH, BF16), sds(RET_WIDTH, F32)],
        compiler_params=_cparams("arbitrary", "arbitrary"),
        name="combine_inproj",
    )(dest, x, route, g2, ys, sc, sh, g, w_bf16, cos_t, sin_t)


def _rope_tables(s):
    half = HEAD_DIM // 2
    inv_freq = ROPE_BASE ** (-jnp.arange(half, dtype=F32) / half)
    ang = jnp.arange(s, dtype=F32)[:, None] * inv_freq[None, :]
    cos = jnp.tile(jnp.cos(ang), (1, 2 * RET_HEADS))
    sin = jnp.sin(ang)
    sin = jnp.tile(jnp.concatenate([-sin, sin], axis=1), (1, RET_HEADS))
    return cos, sin


def _trunk(x, c, norm1_g, norm2_g, w_ada, b_ada, w_in, attn_rel_bias, ssm_a_re, ssm_a_im, ssm_log_dt,
           ssm_b_re, ssm_b_im, ssm_c_re, ssm_c_im, ssm_d, ssm_w_glu, ssm_b_glu, ret_gn_g, w_out,
           moe_w_group, moe_b_group, moe_w_expert, moe_b_expert, moe_w_gate, moe_w_up, moe_w_down, final_g,
           *, row_tile):
    b, s, d = x.shape
    n_layers = w_in.shape[0]
    n_slots = 2 * b * s + N_EXPERTS * MOE_TM
    mod = _adaln(c, w_ada, b_ada).reshape(n_layers, b, 6, 1, d)
    cos_t, sin_t = _rope_tables(s)
    ret_tables = _ret_tables(RET_T)
    w_in_b = w_in.astype(BF16)
    w_out_b = w_out.astype(BF16)
    n_route = N_GROUPS + N_EXPERTS
    w_route = jnp.pad(jnp.concatenate([moe_w_group, moe_w_expert], axis=-1).astype(F32),
                      ((0, 0), (0, 0), (0, LANES - n_route)))
    w_route_hi = w_route.astype(BF16)
    w_route = jnp.stack([w_route_hi, (w_route - w_route_hi.astype(F32)).astype(BF16)], axis=1)
    b_route = jnp.pad(jnp.concatenate([moe_b_group, moe_b_expert], axis=-1).astype(F32),
                      ((0, 0), (0, LANES - n_route))).reshape(n_layers, 1, LANES)
    fg = final_g.reshape(1, d).astype(F32)
    projected = _inproj(x, mod[0, :, 1], mod[0, :, 0], norm1_g[0].reshape(1, d), w_in_b[0], cos_t, sin_t, row_tile)
    for i in range(n_layers):
        _, _, g1, sh2, sc2, g2 = (mod[i, :, k] for k in range(6))
        qa, ka, va, us, qr, kr, vr, gr = projected
        y_a = _attention(qa, ka, va, _attn_bias_table(attn_rel_bias[i], ATTN_TQ))
        bm, cm, ar, ai = _ssm_params(ssm_a_re[i], ssm_a_im[i], ssm_log_dt[i], ssm_b_re[i], ssm_b_im[i],
                                     ssm_c_re[i], ssm_c_im[i])
        y_s = _ssm(us, b, bm, cm, ar, ai, ssm_d[i], ssm_w_glu[i], ssm_b_glu[i])
        y_r = _retention(qr, kr, vr, gr, ret_gn_g[i], ret_tables)
        x, h2, route = _outproj_router(x, y_a, y_s, y_r, w_out_b[i], g1, sc2, sh2,
                                       norm2_g[i].reshape(1, d), w_route[i], b_route[i], row_tile)
        expert_ids = route[:, :, 0:2].astype(jnp.int32).reshape(b * s, 2)
        dest, tile_expert, fill, meta = _slot_plan(expert_ids, n_slots)
        xs = _dispatch(h2, dest, fill, meta, n_slots)
        ys = _experts(xs, tile_expert, meta, moe_w_gate, moe_w_up, moe_w_down, i)
        if i == n_layers - 1:
            x = _combine(x, route, g2, fg, ys, dest, 256, final_norm=True)
        else:
            x, *projected = _combine_inproj(x, route, g2, ys, dest, mod[i + 1, :, 1], mod[i + 1, :, 0],
                                            norm1_g[i + 1].reshape(1, d), w_in_b[i + 1], cos_t, sin_t, row_tile)
    return x


def kernel(x, c, norm1_g, norm2_g, w_ada, b_ada, w_in, attn_rel_bias, ssm_a_re, ssm_a_im, ssm_log_dt, ssm_b_re, ssm_b_im, ssm_c_re, ssm_c_im, ssm_d, ssm_w_glu, ssm_b_glu, ret_gn_g, w_out, moe_w_group, moe_b_group, moe_w_expert, moe_b_expert, moe_w_gate, moe_w_up, moe_w_down, final_g):
    return _trunk(x, c, norm1_g, norm2_g, w_ada, b_ada, w_in, attn_rel_bias, ssm_a_re, ssm_a_im, ssm_log_dt,
                  ssm_b_re, ssm_b_im, ssm_c_re, ssm_c_im, ssm_d, ssm_w_glu, ssm_b_glu, ret_gn_g, w_out,
                  moe_w_group, moe_b_group, moe_w_expert, moe_b_expert, moe_w_gate, moe_w_up, moe_w_down,
                  final_g, row_tile=512)
```

```python
import functools
import math

import jax
import jax.numpy as jnp
import numpy as np
from jax import lax
from jax.experimental import pallas as pl
from jax.experimental.pallas import tpu as pltpu

F32 = jnp.float32
BF16 = jnp.bfloat16

D_MODEL = 1024
N_LAYERS = 4
CHUNK = 64
HEAD_DIM = 64
NORM_EPS = 1e-6

ATTN_HEADS = 8
ATTN_WIDTH = ATTN_HEADS * HEAD_DIM
LEFT_CHUNKS = 8
MAX_REL = 128
REL_TABLE = MAX_REL + CHUNK

SSM_WIDTH = 256
SSM_GROUP = 16
SSM_GROUPS = 16
SSM_STATE = 64
SSM_COMPLEX = SSM_GROUPS * SSM_STATE

RET_HEADS = 4
RET_WIDTH = RET_HEADS * HEAD_DIM
ROPE_BASE = 10000.0

N_GROUPS = 4
EXPERTS_PER_GROUP = 8
N_EXPERTS = N_GROUPS * EXPERTS_PER_GROUP
EXPERT_FF = 256

SUBLANES = 8
LANES = 128
VMEM_LIMIT = 56 * 1024 * 1024

ATTN_TQ = 256
RET_T = 256
SSM_TS = 64
MOE_TM = 512
ROW_TILES = D_MODEL // LANES
NEG_BIG = -1e30
LOG2_E = math.log2(math.e)


def _sigmoid(x):
    return 1.0 / (1.0 + jnp.exp(-x))


def _cparams(*sem):
    return pltpu.CompilerParams(dimension_semantics=sem, vmem_limit_bytes=VMEM_LIMIT)


def _adaln_body(c_ref, w_ref, b_ref, o_ref):
    c = c_ref[...]
    cond = c * _sigmoid(c)
    o_ref[...] = jnp.dot(cond, w_ref[...], preferred_element_type=F32,
                         precision=lax.Precision.HIGHEST) + b_ref[...]


def _adaln(c, w_ada, b_ada):
    n_l, d, d6 = w_ada.shape
    b = c.shape[0]
    tn = 1536
    return pl.pallas_call(
        _adaln_body,
        grid=(n_l, d6 // tn),
        in_specs=[pl.BlockSpec((b, d), lambda l, j: (0, 0)),
                  pl.BlockSpec((None, d, tn), lambda l, j: (l, 0, j)),
                  pl.BlockSpec((None, 1, tn), lambda l, j: (l, 0, j))],
        out_specs=pl.BlockSpec((None, b, tn), lambda l, j: (l, 0, j)),
        out_shape=jax.ShapeDtypeStruct((n_l, b, d6), F32),
        compiler_params=_cparams("arbitrary", "arbitrary"),
        name="adaln",
    )(c, w_ada, b_ada.reshape(n_l, 1, d6))


def _modulated_norm(x, g, sc, sh):
    ms = jnp.mean(x * x, axis=-1, keepdims=True)
    return (x * lax.rsqrt(ms + NORM_EPS) * g) * (1.0 + sc) + sh


def _inproj_body(x_ref, *refs):
    _inproj_project(x_ref[...], *refs)


def _inproj_project(x, sc_ref, sh_ref, g_ref, w_ref, cos_ref, sin_ref,
                    qa_ref, ka_ref, va_ref, us_ref, qr_ref, kr_ref, vr_ref, gr_ref, between=None):
    hb = _modulated_norm(x, g_ref[...], sc_ref[...], sh_ref[...]).astype(BF16)
    stage = iter(range(INPROJ_STAGES))

    def proj(lo, width):
        if between is not None:
            between(next(stage))
        return jnp.dot(hb, w_ref[:, lo:lo + width], preferred_element_type=F32)

    scale = HEAD_DIM ** -0.5
    a = ATTN_WIDTH
    qa_ref[...] = (proj(0, a) * (scale * LOG2_E)).astype(BF16)
    ka_ref[...] = proj(a, a).astype(BF16)
    va_ref[...] = proj(2 * a, a).astype(BF16)
    o = 3 * a
    us_ref[...] = proj(o, SSM_WIDTH)
    o += SSM_WIDTH

    cos = cos_ref[...]
    sin = sin_ref[...]
    lane = lax.broadcasted_iota(jnp.int32, cos.shape, 1)
    first_half = (lane & (HEAD_DIM // 2)) == 0

    def rotary(z):
        partner = jnp.where(first_half,
                            pltpu.roll(z, RET_WIDTH - HEAD_DIM // 2, 1),
                            pltpu.roll(z, HEAD_DIM // 2, 1))
        return z * cos + partner * sin

    r = RET_WIDTH
    qr_ref[...] = rotary(proj(o, r)).astype(BF16)
    kr_ref[...] = (rotary(proj(o + r, r)) * scale).astype(BF16)
    vr_ref[...] = proj(o + 2 * r, r).astype(BF16)
    gr_ref[...] = proj(o + 3 * r, r)


def _inproj(x, sc, sh, g, w_bf16, cos_t, sin_t, ts):
    b, s, d = x.shape
    nj = s // ts
    tok = lambda width: pl.BlockSpec((None, ts, width), lambda bi, j: (bi, j, 0))
    per_b = pl.BlockSpec((None, 1, d), lambda bi, j: (bi, 0, 0))
    rope = pl.BlockSpec((ts, RET_WIDTH), lambda bi, j: (j, 0))
    sds = lambda width, dt: jax.ShapeDtypeStruct((b, s, width), dt)
    return pl.pallas_call(
        _inproj_body,
        grid=(b, nj),
        in_specs=[tok(d), per_b, per_b,
                  pl.BlockSpec((1, d), lambda bi, j: (0, 0)),
                  pl.BlockSpec(w_bf16.shape, lambda bi, j: (0, 0)),
                  rope, rope],
        out_specs=[tok(ATTN_WIDTH), tok(ATTN_WIDTH), tok(ATTN_WIDTH),
                   pl.BlockSpec((ts, SSM_WIDTH), lambda bi, j: (j, bi)),
                   tok(RET_WIDTH), tok(RET_WIDTH), tok(RET_WIDTH), tok(RET_WIDTH)],
        out_shape=[sds(ATTN_WIDTH, BF16), sds(ATTN_WIDTH, BF16), sds(ATTN_WIDTH, BF16),
                   jax.ShapeDtypeStruct((s, b * SSM_WIDTH), F32),
                   sds(RET_WIDTH, BF16), sds(RET_WIDTH, BF16), sds(RET_WIDTH, BF16), sds(RET_WIDTH, F32)],
        compiler_params=_cparams("parallel", "arbitrary"),
        name="inproj",
    )(x, sc, sh, g, w_bf16, cos_t, sin_t)


def _attn_body(q_ref, k0_ref, k1_ref, k2_ref, v0_ref, v1_ref, v2_ref, bias_ref, o_ref):
    j = pl.program_id(1)
    tq = q_ref.shape[0]
    lane = lax.broadcasted_iota(jnp.int32, (1, LANES), 1)
    low = lane < HEAD_DIM
    ones_cols = jnp.ones((3 * tq, LANES), BF16)

    def heads(mask_start):
        if mask_start:
            col = lax.broadcasted_iota(jnp.int32, (tq, 3 * tq), 1)
            in_seq = col >= (2 - j) * tq
        for hp in range(ATTN_HEADS // 2):
            sl = slice(hp * LANES, (hp + 1) * LANES)
            q2 = q_ref[:, sl]
            k2 = jnp.concatenate([k0_ref[:, sl], k1_ref[:, sl], k2_ref[:, sl]], axis=0)
            v2 = jnp.concatenate([v0_ref[:, sl], v1_ref[:, sl], v2_ref[:, sl]], axis=0)
            v_aug = jnp.concatenate([v2, ones_cols], axis=1)
            outs = []
            for sub in range(2):
                keep = low if sub == 0 else jnp.logical_not(low)
                kz = jnp.where(keep, k2, jnp.zeros_like(k2))
                s = lax.dot_general(q2, kz, (((1,), (1,)), ((), ())), preferred_element_type=F32)
                s = s + bias_ref[2 * hp + sub]
                if mask_start:
                    s = jnp.where(in_seq, s, NEG_BIG)
                p = jnp.exp2(s - jnp.max(s, axis=-1, keepdims=True)).astype(BF16)
                r = jnp.dot(p, v_aug, preferred_element_type=F32)
                outs.append(r[:, :LANES] * (1.0 / r[:, LANES:LANES + 1]))
            o_ref[:, sl] = jnp.where(low, outs[0], outs[1]).astype(BF16)

    @pl.when(j < 2)
    def _():
        heads(True)

    @pl.when(j >= 2)
    def _():
        heads(False)


def _attn_bias_table(rel_bias, tq):
    r = np.arange(tq)[:, None]
    c = np.arange(3 * tq)[None, :]
    qc = r // CHUNK + 2 * tq // CHUNK
    kc = c // CHUNK
    band = (kc >= qc - LEFT_CHUNKS) & (kc <= qc)
    period = 4 * tq
    k = np.arange(period)
    diff = np.where(k < 3 * tq, k, k - period)
    idx = np.clip(diff - 2 * tq, -MAX_REL, CHUNK - 1) + MAX_REL
    vec = rel_bias.astype(F32)[:, idx]
    h = vec.shape[0]
    table = jnp.tile(vec, (1, tq))[:, :tq * (period - 1)].reshape(h, tq, period - 1)[:, :, :3 * tq]
    return jnp.where(jnp.asarray(band)[None], table * LOG2_E, NEG_BIG)


def _attention(qa, ka, va, bias):
    b, s, w = qa.shape
    tq = ATTN_TQ
    qspec = pl.BlockSpec((None, tq, w), lambda bi, j: (bi, j, 0))
    kspec = lambda back: pl.BlockSpec((None, tq, w), lambda bi, j: (bi, jnp.maximum(j - back, 0), 0))
    return pl.pallas_call(
        _attn_body,
        grid=(b, s // tq),
        in_specs=[qspec, kspec(2), kspec(1), kspec(0), kspec(2), kspec(1), kspec(0),
                  pl.BlockSpec(bias.shape, lambda bi, j: (0, 0, 0))],
        out_specs=qspec,
        out_shape=jax.ShapeDtypeStruct((b, s, w), BF16),
        compiler_params=_cparams("parallel", "arbitrary"),
        name="band_attn",
    )(qa, ka, ka, ka, va, va, va, bias)


def _gelu_tanh(x):
    return 0.5 * x * (1.0 + jnp.tanh(math.sqrt(2.0 / math.pi) * (x + 0.044715 * (x * x * x))))


def _ssm_body(u_ref, bm_ref, cm_ref, ar_ref, ai_ref, d_ref, wg_ref, bg_ref, y_ref,
              st_ref, bu_ref, ut_ref, yt_ref):
    nb = st_ref.shape[1]
    ts = u_ref.shape[0]
    nc = SSM_COMPLEX
    w = SSM_WIDTH
    w_ct = w // LANES

    @pl.when(pl.program_id(0) == 0)
    def _():
        st_ref[...] = jnp.zeros_like(st_ref)

    for bi in range(nb):
        for ct in range(w_ct):
            lo = bi * w + ct * LANES
            ut_ref.at[ct][pl.ds(bi, ts, stride=nb), :] = u_ref[:, lo:lo + LANES]
    u_tb = jnp.concatenate([ut_ref[ct] for ct in range(w_ct)], axis=1)
    n_ct = 2 * nc // LANES
    bu = jnp.dot(u_tb.astype(BF16), bm_ref[...], preferred_element_type=F32)
    for ct in range(n_ct):
        bu_ref[ct] = bu[:, ct * LANES:(ct + 1) * LANES]

    tiles_per_block = 4
    half = nc // LANES
    for t0 in range(0, half, tiles_per_block):
        tiles = range(t0, t0 + tiles_per_block)
        ar = [jnp.broadcast_to(ar_ref[:, ct * LANES:(ct + 1) * LANES], (nb, LANES)) for ct in tiles]
        ai = [jnp.broadcast_to(ai_ref[:, ct * LANES:(ct + 1) * LANES], (nb, LANES)) for ct in tiles]

        def step(t, carry):
            rows = pl.ds(pl.multiple_of(t * nb, nb), nb)
            out = []
            for k, ct in enumerate(tiles):
                xr, xi = carry[2 * k], carry[2 * k + 1]
                nr = ar[k] * xr - ai[k] * xi + bu_ref[ct, rows, :]
                ni = ar[k] * xi + ai[k] * xr + bu_ref[half + ct, rows, :]
                bu_ref[ct, rows, :] = nr
                bu_ref[half + ct, rows, :] = ni
                out += [nr, ni]
            return tuple(out)

        init = []
        for ct in tiles:
            init += [st_ref[0, :, ct * LANES:(ct + 1) * LANES], st_ref[1, :, ct * LANES:(ct + 1) * LANES]]
        fin = lax.fori_loop(0, ts, step, tuple(init), unroll=4)
        for k, ct in enumerate(tiles):
            st_ref[0, :, ct * LANES:(ct + 1) * LANES] = fin[2 * k]
            st_ref[1, :, ct * LANES:(ct + 1) * LANES] = fin[2 * k + 1]

    states = jnp.concatenate([bu_ref[ct] for ct in range(n_ct)], axis=1).astype(BF16)
    u_tb = jnp.concatenate([ut_ref[ct] for ct in range(w_ct)], axis=1)
    y = _gelu_tanh(jnp.dot(states, cm_ref[...], preferred_element_type=F32) + d_ref[...] * u_tb)
    z = jnp.dot(y.astype(BF16), wg_ref[...], preferred_element_type=F32) + bg_ref[...]
    y = y * _sigmoid(z)
    for ct in range(w_ct):
        yt_ref[ct] = y[:, ct * LANES:(ct + 1) * LANES]
    for bi in range(nb):
        for ct in range(w_ct):
            lo = bi * w + ct * LANES
            y_ref[:, lo:lo + LANES] = yt_ref.at[ct][pl.ds(bi, ts, stride=nb), :].astype(BF16)


def _ssm_params(a_re, a_im, log_dt, b_re, b_im, c_re, c_im):
    g, p, cg = b_re.shape
    lam = lax.complex(a_re.astype(F32), a_im.astype(F32))
    dt = jnp.exp(log_dt.astype(F32))[:, None]
    a_bar = jnp.exp(lam * dt)
    b_bar = ((a_bar - 1.0) / lam)[:, :, None] * lax.complex(b_re.astype(F32), b_im.astype(F32))
    eye = jnp.eye(g, dtype=F32)
    b_blk = lambda m: jnp.einsum('gpc,gh->gchp', m, eye).reshape(g * cg, g * p)
    bm = jnp.concatenate([b_blk(jnp.real(b_bar)), b_blk(jnp.imag(b_bar))], axis=1)
    c_blk = lambda m: jnp.einsum('gcp,gh->gphc', m.astype(F32), eye).reshape(g * p, g * cg)
    cm = jnp.concatenate([c_blk(c_re), -c_blk(c_im)], axis=0)
    ar = jnp.real(a_bar).reshape(1, g * p)
    ai = jnp.imag(a_bar).reshape(1, g * p)
    return bm.astype(BF16), cm.astype(BF16), ar, ai


def _ssm(us, nb, bm, cm, ar, ai, d_skip, w_glu, b_glu):
    s_len, wide = us.shape
    w = SSM_WIDTH
    ts = SSM_TS
    full = lambda a: pl.BlockSpec(a.shape, lambda i: (0,) * a.ndim)
    d2 = d_skip.reshape(1, w).astype(F32)
    bg2 = b_glu.reshape(1, w).astype(F32)
    wg = w_glu.astype(BF16)
    return pl.pallas_call(
        _ssm_body,
        grid=(s_len // ts,),
        in_specs=[pl.BlockSpec((ts, wide), lambda i: (i, 0)),
                  full(bm), full(cm), full(ar), full(ai), full(d2), full(wg), full(bg2)],
        out_specs=pl.BlockSpec((ts, wide), lambda i: (i, 0)),
        out_shape=jax.ShapeDtypeStruct((s_len, wide), BF16),
        scratch_shapes=[pltpu.VMEM((2, nb, SSM_COMPLEX), F32),
                        pltpu.VMEM((2 * SSM_COMPLEX // LANES, ts * nb, LANES), F32),
                        pltpu.VMEM((w // LANES, ts * nb, LANES), F32),
                        pltpu.VMEM((w // LANES, ts * nb, LANES), F32)],
        compiler_params=_cparams("arbitrary"),
        name="s5",
    )(us, bm, cm, ar, ai, d2, wg, bg2)


def _ret_body(q_ref, k_ref, v_ref, g_ref, gn_ref, dec_ref, xi_ref, zeta_ref, cd_ref, avg_ref, o_ref, st_ref):
    @pl.when(pl.program_id(1) == 0)
    def _():
        st_ref[...] = jnp.zeros_like(st_ref)

    lane = lax.broadcasted_iota(jnp.int32, (1, LANES), 1)
    low = lane < HEAD_DIM
    avg = avg_ref[...]

    def head_mean(t):
        hi = t.astype(BF16)
        lo = (t - hi.astype(F32)).astype(BF16)
        return jnp.dot(hi, avg, preferred_element_type=F32) + jnp.dot(lo, avg, preferred_element_type=F32)

    for hp in range(RET_HEADS // 2):
        sl = slice(hp * LANES, (hp + 1) * LANES)
        q2 = q_ref[:, sl]
        k2 = k_ref[:, sl]
        v2 = v_ref[:, sl]
        parts = []
        for sub in range(2):
            keep = low if sub == 0 else jnp.logical_not(low)
            k_one = jnp.where(keep, k2, jnp.zeros_like(k2))
            a = lax.dot_general(q2, k_one, (((1,), (1,)), ((), ())), preferred_element_type=F32)
            a = a * dec_ref[2 * hp + sub]
            parts.append(jnp.dot(a.astype(BF16), v2, preferred_element_type=F32))
        inner = jnp.where(low, parts[0], parts[1])
        st = st_ref[hp]
        qx = (q2.astype(F32) * xi_ref[hp]).astype(BF16)
        y = inner + jnp.dot(qx, st.astype(BF16), preferred_element_type=F32)
        kz = (k2.astype(F32) * zeta_ref[hp]).astype(BF16)
        kv = lax.dot_general(kz, v2, (((0,), (0,)), ((), ())), preferred_element_type=F32)
        blocks = cd_ref[hp]
        st_ref[hp] = blocks * st + jnp.where(blocks > 0.0, kv, 0.0)
        yc = y - head_mean(y)
        var = head_mean(yc * yc)
        yn = yc * lax.rsqrt(var + NORM_EPS) * gn_ref[:, sl]
        g = g_ref[:, sl]
        o_ref[:, sl] = (g * _sigmoid(g) * yn).astype(BF16)


def _ret_tables(t_len):
    n_pairs = RET_HEADS // 2
    log_gamma = jnp.log1p(-jnp.exp2(-5.0 - jnp.arange(RET_HEADS, dtype=F32)))
    t = jnp.arange(t_len, dtype=F32)
    diff = t[:, None] - t[None, :]
    dec = jnp.where(diff >= 0, jnp.exp(log_gamma[:, None, None] * jnp.maximum(diff, 0.0)), 0.0)
    per_lane = lambda m: jnp.repeat(m.reshape(n_pairs, 2, -1), HEAD_DIM, axis=1).transpose(0, 2, 1)
    xi = per_lane(jnp.exp(log_gamma[:, None] * (t + 1.0)))
    zeta = per_lane(jnp.exp(log_gamma[:, None] * (t_len - 1 - t)))
    lane_head = jnp.arange(LANES) // HEAD_DIM
    same = (lane_head[:, None] == lane_head[None, :]).astype(F32)
    cd_lane = jnp.repeat(jnp.exp(log_gamma * t_len).reshape(n_pairs, 2), HEAD_DIM, axis=1)
    cd = cd_lane[:, :, None] * same[None]
    avg = (same / HEAD_DIM).astype(BF16)
    return dec, xi, zeta, cd, avg


def _retention(qr, kr, vr, gr, gn_g, tables):
    b, s, w = qr.shape
    t_len = RET_T
    dec, xi, zeta, cd, avg = tables
    tok = pl.BlockSpec((None, t_len, w), lambda bi, j: (bi, j, 0))
    full = lambda a: pl.BlockSpec(a.shape, lambda bi, j: (0,) * a.ndim)
    gn2 = gn_g.reshape(1, w).astype(F32)
    return pl.pallas_call(
        _ret_body,
        grid=(b, s // t_len),
        in_specs=[tok, tok, tok, tok, full(gn2), full(dec), full(xi), full(zeta), full(cd), full(avg)],
        out_specs=tok,
        out_shape=jax.ShapeDtypeStruct((b, s, w), BF16),
        scratch_shapes=[pltpu.VMEM((RET_HEADS // 2, LANES, LANES), F32)],
        compiler_params=_cparams("parallel", "arbitrary"),
        name="retention",
    )(qr, kr, vr, gr, gn2, dec, xi, zeta, cd, avg)


def _store_row_tiled(ref, val):
    rows = val.shape[0]
    for t in range(ROW_TILES):
        ref[pl.ds(t, rows, stride=ROW_TILES), :] = val[:, t * LANES:(t + 1) * LANES]


def _load_row_tiled(ref, first, rows):
    return jnp.concatenate(
        [ref[pl.ds(first * ROW_TILES + t, rows, stride=ROW_TILES), :] for t in range(ROW_TILES)], axis=1)


def _outproj_body(x_ref, ya_ref, ys_ref, yr_ref, w_ref, g1_ref, sc_ref, sh_ref, ng_ref, wr_ref, br_ref,
                  xo_ref, h2_ref, route_ref):
    a = ATTN_WIDTH
    mixed = jnp.dot(ya_ref[...], w_ref[0:a, :], preferred_element_type=F32)
    mixed += jnp.dot(ys_ref[...], w_ref[a:a + SSM_WIDTH, :], preferred_element_type=F32)
    mixed += jnp.dot(yr_ref[...], w_ref[a + SSM_WIDTH:, :], preferred_element_type=F32)
    x = x_ref[...] + g1_ref[...] * mixed
    xo_ref[...] = x
    h2 = _modulated_norm(x, ng_ref[...], sc_ref[...], sh_ref[...])
    _store_row_tiled(h2_ref, h2)

    h_hi = h2.astype(BF16)
    h_lo = (h2 - h_hi.astype(F32)).astype(BF16)
    logits = (jnp.dot(h_hi, wr_ref[0], preferred_element_type=F32)
              + (jnp.dot(h_hi, wr_ref[1], preferred_element_type=F32)
                 + jnp.dot(h_lo, wr_ref[0], preferred_element_type=F32))
              + br_ref[...])
    lane = lax.broadcasted_iota(jnp.int32, logits.shape, 1).astype(F32)
    big = float(LANES)

    def first_argmax(vals):
        m = jnp.max(vals, axis=-1, keepdims=True)
        return m, jnp.min(jnp.where(vals == m, lane, big), axis=-1, keepdims=True)

    gl = jnp.where(lane < N_GROUPS, logits, -jnp.inf)
    gmax, gidx = first_argmax(gl)
    group_w = 1.0 / jnp.sum(jnp.exp(gl - gmax), axis=-1, keepdims=True)
    e_lo = N_GROUPS + gidx * EXPERTS_PER_GROUP
    el = jnp.where((lane >= e_lo) & (lane < e_lo + EXPERTS_PER_GROUP), logits, -jnp.inf)
    v1, i1 = first_argmax(el)
    v2, i2 = first_argmax(jnp.where(lane == i1, -jnp.inf, el))
    t2 = jnp.exp(v2 - v1)
    w1 = group_w / (1.0 + t2)
    w2 = group_w * t2 / (1.0 + t2)
    route = jnp.where(lane == 0, i1 - N_GROUPS,
                      jnp.where(lane == 1, i2 - N_GROUPS,
                                jnp.where(lane == 2, w1, jnp.where(lane == 3, w2, 0.0))))
    route_ref[...] = route


def _outproj_router(x, ya, ys_tm, yr, w_out_bf16, g1, sc2, sh2, ng, w_route, b_route, ts):
    b, s, d = x.shape
    tok = lambda width: pl.BlockSpec((None, ts, width), lambda bi, j: (bi, j, 0))
    per_b = pl.BlockSpec((None, 1, d), lambda bi, j: (bi, 0, 0))
    full = lambda a: pl.BlockSpec(a.shape, lambda bi, j: (0,) * a.ndim)
    nj = s // ts
    return pl.pallas_call(
        _outproj_body,
        grid=(b, nj),
        in_specs=[tok(d), tok(ATTN_WIDTH),
                  pl.BlockSpec((ts, SSM_WIDTH), lambda bi, j: (j, bi)),
                  tok(RET_WIDTH), full(w_out_bf16), per_b, per_b, per_b, full(ng),
                  full(w_route), full(b_route)],
        out_specs=[tok(d),
                   pl.BlockSpec((ts * ROW_TILES, LANES), lambda bi, j: (bi * nj + j, 0)),
                   pl.BlockSpec((None, ts, LANES), lambda bi, j: (bi, j, 0))],
        out_shape=[jax.ShapeDtypeStruct((b, s, d), F32),
                   jax.ShapeDtypeStruct((b * s * ROW_TILES, LANES), F32),
                   jax.ShapeDtypeStruct((b, s, LANES), F32)],
        compiler_params=_cparams("parallel", "arbitrary"),
        name="outproj_router",
    )(x, ya, ys_tm, yr, w_out_bf16, g1, sc2, sh2, ng, w_route, b_route)


def _slot_plan(expert_ids, n_slots):
    tm = MOE_TM
    e = expert_ids.reshape(-1)
    onehot = (e[:, None] == jnp.arange(N_EXPERTS, dtype=jnp.int32)[None, :]).astype(jnp.int32)
    csum = jnp.cumsum(onehot, axis=0)
    rank = jnp.sum(onehot * (csum - 1), axis=1)
    counts = csum[-1]
    padded = ((counts + tm - 1) // tm) * tm
    ends = jnp.cumsum(padded)
    starts = ends - padded
    dest = (jnp.sum(onehot * starts[None, :], axis=1) + rank).astype(jnp.int32)
    n_tiles = n_slots // tm
    tile_row = jnp.arange(n_tiles, dtype=jnp.int32) * tm
    tile_expert = jnp.minimum(jnp.sum((tile_row[:, None] >= ends[None, :]).astype(jnp.int32), axis=1),
                              N_EXPERTS - 1).astype(jnp.int32)
    used_tiles = (ends[-1] // tm).astype(jnp.int32)
    group_last = jnp.where(padded > 0, ends - tm, n_slots)
    tail = jnp.where(tile_row >= ends[-1], tile_row, n_slots)
    fill = jnp.sort(jnp.concatenate([group_last, tail]).astype(jnp.int32))
    n_fill = jnp.sum(fill < n_slots).astype(jnp.int32)
    meta = jnp.stack([n_fill, used_tiles]).astype(jnp.int32)
    return dest, tile_expert, fill, meta


INPROJ_STAGES = 8
DMA_UNROLL = 8


def _dispatch_body(dest_ref, fill_ref, meta_ref, h2_ref, xs_hbm, zero_ref, sem):
    step = pl.program_id(0)
    tile_rows = MOE_TM * ROW_TILES
    tokens = h2_ref.shape[0] // ROW_TILES

    def fill_copy(i):
        row = pl.multiple_of(fill_ref[i] * ROW_TILES, tile_rows)
        return pltpu.make_async_copy(zero_ref, xs_hbm.at[pl.ds(row, tile_rows), :], sem)

    @pl.when(step == 0)
    def _():
        zero_ref[...] = jnp.zeros_like(zero_ref)
        n_fill = meta_ref[0]
        lax.fori_loop(0, n_fill, lambda i, c: (fill_copy(i).start(), c)[1], 0)
        lax.fori_loop(0, n_fill, lambda i, c: (fill_copy(i).wait(), c)[1], 0)

    base = step * (2 * tokens)

    def row_copy(i, choice):
        src = pl.multiple_of(i * ROW_TILES, ROW_TILES)
        dst = pl.multiple_of(dest_ref[base + 2 * i + choice] * ROW_TILES, ROW_TILES)
        return pltpu.make_async_copy(h2_ref.at[pl.ds(src, ROW_TILES), :],
                                     xs_hbm.at[pl.ds(dst, ROW_TILES), :], sem)

    def start(i, c):
        row_copy(i, 0).start(priority=0)
        row_copy(i, 1).start(priority=1)
        return c

    def wait(i, c):
        row_copy(i, 0).wait()
        row_copy(i, 1).wait()
        return c

    lax.fori_loop(0, tokens, start, 0, unroll=DMA_UNROLL)
    lax.fori_loop(0, tokens, wait, 0, unroll=DMA_UNROLL)


def _dispatch(h2_tiled, dest, fill, meta, n_slots):
    tokens = 512
    rows = tokens * ROW_TILES
    return pl.pallas_call(
        _dispatch_body,
        grid_spec=pltpu.PrefetchScalarGridSpec(
            num_scalar_prefetch=3,
            grid=(h2_tiled.shape[0] // rows,),
            in_specs=[pl.BlockSpec((rows, LANES), lambda i, d, f, m: (i, 0))],
            out_specs=pl.BlockSpec(memory_space=pl.ANY),
            scratch_shapes=[pltpu.VMEM((MOE_TM * ROW_TILES, LANES), F32), pltpu.SemaphoreType.DMA],
        ),
        out_shape=jax.ShapeDtypeStruct((n_slots * ROW_TILES, LANES), F32),
        compiler_params=pltpu.CompilerParams(dimension_semantics=("arbitrary",), has_side_effects=True,
                                             vmem_limit_bytes=VMEM_LIMIT),
        name="moe_dispatch",
    )(dest, fill, meta, h2_tiled)


def _experts_body(te_ref, meta_ref, xs_ref, wg_ref, wu_ref, wd_ref, ys_ref):
    tm = MOE_TM

    @pl.when(pl.program_id(0) < meta_ref[1])
    def _():
        x = _load_row_tiled(xs_ref, 0, tm).astype(BF16)
        hg = jnp.dot(x, wg_ref[...].astype(BF16), preferred_element_type=F32)
        hu = jnp.dot(x, wu_ref[...].astype(BF16), preferred_element_type=F32)
        act = (hg * _sigmoid(hg) * hu).astype(BF16)
        _store_row_tiled(ys_ref, jnp.dot(act, wd_ref[...].astype(BF16), preferred_element_type=F32))

    @pl.when(pl.program_id(0) >= meta_ref[1])
    def _():
        ys_ref[...] = jnp.zeros_like(ys_ref)


def _experts(xs, tile_expert, meta, w_gate, w_up, w_down, layer):
    n_tiles = tile_expert.shape[0]
    rows = MOE_TM * ROW_TILES
    d, f = w_gate.shape[2], w_gate.shape[3]
    return pl.pallas_call(
        _experts_body,
        grid_spec=pltpu.PrefetchScalarGridSpec(
            num_scalar_prefetch=2,
            grid=(n_tiles,),
            in_specs=[pl.BlockSpec((rows, LANES), lambda i, te, mt: (i, 0)),
                      pl.BlockSpec((None, None, d, f), lambda i, te, mt: (layer, te[i], 0, 0)),
                      pl.BlockSpec((None, None, d, f), lambda i, te, mt: (layer, te[i], 0, 0)),
                      pl.BlockSpec((None, None, f, d), lambda i, te, mt: (layer, te[i], 0, 0))],
            out_specs=pl.BlockSpec((rows, LANES), lambda i, te, mt: (i, 0)),
        ),
        out_shape=jax.ShapeDtypeStruct(xs.shape, F32),
        compiler_params=_cparams("arbitrary"),
        name="moe_experts",
    )(tile_expert, meta, xs, w_gate, w_up, w_down)


def _combine_body(dest_ref, x_ref, route_ref, g2_ref, fg_ref, ys_hbm, o_ref, buf_ref, sem, *, final_norm):
    tm = x_ref.shape[0]
    base = (pl.program_id(0) * pl.num_programs(1) + pl.program_id(1)) * tm

    def row_copy(i, choice):
        src = pl.multiple_of(dest_ref[(base + i) * 2 + choice] * ROW_TILES, ROW_TILES)
        dst = pl.multiple_of((choice * tm + i) * ROW_TILES, ROW_TILES)
        return pltpu.make_async_copy(ys_hbm.at[pl.ds(src, ROW_TILES), :],
                                     buf_ref.at[pl.ds(dst, ROW_TILES), :], sem)

    def start(i, c):
        row_copy(i, 0).start(priority=0)
        row_copy(i, 1).start(priority=1)
        return c

    def wait(i, c):
        row_copy(i, 0).wait()
        row_copy(i, 1).wait()
        return c

    lax.fori_loop(0, tm, start, 0, unroll=DMA_UNROLL)
    lax.fori_loop(0, tm, wait, 0, unroll=DMA_UNROLL)
    route = route_ref[...]
    y = route[:, 2:3] * _load_row_tiled(buf_ref, 0, tm) + route[:, 3:4] * _load_row_tiled(buf_ref, tm, tm)
    x = x_ref[...] + g2_ref[...] * y
    if final_norm:
        ms = jnp.mean(x * x, axis=-1, keepdims=True)
        x = x * lax.rsqrt(ms + NORM_EPS) * fg_ref[...]
    o_ref[...] = x


def _combine(x, route, g2, final_g, ys, dest, tm, final_norm):
    b, s, d = x.shape
    tok = lambda width: pl.BlockSpec((None, tm, width), lambda bi, j, dst: (bi, j, 0))
    return pl.pallas_call(
        functools.partial(_combine_body, final_norm=final_norm),
        grid_spec=pltpu.PrefetchScalarGridSpec(
            num_scalar_prefetch=1,
            grid=(b, s // tm),
            in_specs=[tok(d), tok(LANES),
                      pl.BlockSpec((None, 1, d), lambda bi, j, dst: (bi, 0, 0)),
                      pl.BlockSpec((1, d), lambda bi, j, dst: (0, 0)),
                      pl.BlockSpec(memory_space=pl.ANY)],
            out_specs=tok(d),
            scratch_shapes=[pltpu.VMEM((2 * tm * ROW_TILES, LANES), F32), pltpu.SemaphoreType.DMA],
        ),
        out_shape=jax.ShapeDtypeStruct((b, s, d), F32),
        compiler_params=_cparams("arbitrary", "arbitrary"),
        name="moe_combine",
    )(dest, x, route, g2, final_g, ys)


def _combine_inproj_body(dest_ref, x_ref, route_ref, g2_ref, ys_hbm, sc_ref, sh_ref, g_ref, w_ref, cos_ref,
                         sin_ref, xo_ref, qa_ref, ka_ref, va_ref, us_ref, qr_ref, kr_ref, vr_ref, gr_ref,
                         buf_ref, sems):
    tm = x_ref.shape[0]
    n_tiles = pl.num_programs(0) * pl.num_programs(1)
    tile = pl.program_id(0) * pl.num_programs(1) + pl.program_id(1)
    slot = tile % 2

    def row_copy(t, s, i, choice):
        src = pl.multiple_of(dest_ref[(t * tm + i) * 2 + choice] * ROW_TILES, ROW_TILES)
        dst = pl.multiple_of(((2 * s + choice) * tm + i) * ROW_TILES, ROW_TILES)
        return pltpu.make_async_copy(ys_hbm.at[pl.ds(src, ROW_TILES), :],
                                     buf_ref.at[pl.ds(dst, ROW_TILES), :], sems.at[s])

    def start_rows(t, s, lo, hi):
        def start(i, c):
            row_copy(t, s, i, 0).start(priority=0)
            row_copy(t, s, i, 1).start(priority=1)
            return c
        lax.fori_loop(lo, hi, start, 0, unroll=DMA_UNROLL)

    @pl.when(tile == 0)
    def _():
        start_rows(tile, slot, 0, tm)

    chunk = tm // INPROJ_STAGES
    nxt = jnp.minimum(tile + 1, n_tiles - 1)

    def prefetch_chunk(k):
        for i in range(k * chunk, (k + 1) * chunk):
            row_copy(nxt, 1 - slot, i, 0).start(priority=0)
            row_copy(nxt, 1 - slot, i, 1).start(priority=1)

    def wait_tile(t, s):
        def wait(i, c):
            row_copy(t, s, i, 0).wait()
            row_copy(t, s, i, 1).wait()
            return c
        lax.fori_loop(0, tm, wait, 0, unroll=DMA_UNROLL)

    wait_tile(tile, slot)
    route = route_ref[...]
    first = 2 * slot * tm
    y = (route[:, 2:3] * _load_row_tiled(buf_ref, first, tm)
         + route[:, 3:4] * _load_row_tiled(buf_ref, first + tm, tm))
    x = x_ref[...] + g2_ref[...] * y
    xo_ref[...] = x
    _inproj_project(x, sc_ref, sh_ref, g_ref, w_ref, cos_ref, sin_ref,
                    qa_ref, ka_ref, va_ref, us_ref, qr_ref, kr_ref, vr_ref, gr_ref, between=prefetch_chunk)

    @pl.when(tile == n_tiles - 1)
    def _():
        wait_tile(nxt, 1 - slot)


def _combine_inproj(x, route, g2, ys, dest, sc, sh, g, w_bf16, cos_t, sin_t, ts):
    b, s, d = x.shape
    tok = lambda width: pl.BlockSpec((None, ts, width), lambda bi, j, dst: (bi, j, 0))
    per_b = pl.BlockSpec((None, 1, d), lambda bi, j, dst: (bi, 0, 0))
    rope = pl.BlockSpec((ts, RET_WIDTH), lambda bi, j, dst: (j, 0))
    sds = lambda width, dt: jax.ShapeDtypeStruct((b, s, width), dt)
    return pl.pallas_call(
        _combine_inproj_body,
        grid_spec=pltpu.PrefetchScalarGridSpec(
            num_scalar_prefetch=1,
            grid=(b, s // ts),
            in_specs=[tok(d), tok(LANES), per_b, pl.BlockSpec(memory_space=pl.ANY), per_b, per_b,
                      pl.BlockSpec((1, d), lambda bi, j, dst: (0, 0)),
                      pl.BlockSpec(w_bf16.shape, lambda bi, j, dst: (0, 0)),
                      rope, rope],
            out_specs=[tok(d), tok(ATTN_WIDTH), tok(ATTN_WIDTH), tok(ATTN_WIDTH),
                       pl.BlockSpec((ts, SSM_WIDTH), lambda bi, j, dst: (j, bi)),
                       tok(RET_WIDTH), tok(RET_WIDTH), tok(RET_WIDTH), tok(RET_WIDTH)],
            scratch_shapes=[pltpu.VMEM((2 * 2 * ts * ROW_TILES, LANES), F32), pltpu.SemaphoreType.DMA((2,))],
        ),
        out_shape=[sds(d, F32), sds(ATTN_WIDTH, BF16), sds(ATTN_WIDTH, BF16), sds(ATTN_WIDTH, BF16),
                   jax.ShapeDtypeStruct((s, b * SSM_WIDTH), F32),
                   sds(RET_WIDTH, BF16), sds(RET_WIDTH, BF16), sds(RET_WIDTH, BF16), sds(RET_WIDTH, F32)],
        compiler_params=_cparams("arbitrary", "arbitrary"),
        name="combine_inproj",
    )(dest, x, route, g2, ys, sc, sh, g, w_bf16, cos_t, sin_t)


def _rope_tables(s):
    half = HEAD_DIM // 2
    inv_freq = ROPE_BASE ** (-jnp.arange(half, dtype=F32) / half)
    ang = jnp.arange(s, dtype=F32)[:, None] * inv_freq[None, :]
    cos = jnp.tile(jnp.cos(ang), (1, 2 * RET_HEADS))
    sin = jnp.sin(ang)
    sin = jnp.tile(jnp.concatenate([-sin, sin], axis=1), (1, RET_HEADS))
    return cos, sin


def _trunk(x, c, norm1_g, norm2_g, w_ada, b_ada, w_in, attn_rel_bias, ssm_a_re, ssm_a_im, ssm_log_dt,
           ssm_b_re, ssm_b_im, ssm_c_re, ssm_c_im, ssm_d, ssm_w_glu, ssm_b_glu, ret_gn_g, w_out,
           moe_w_group, moe_b_group, moe_w_expert, moe_b_expert, moe_w_gate, moe_w_up, moe_w_down, final_g,
           *, row_tile):
    b, s, d = x.shape
    n_layers = w_in.shape[0]
    n_slots = 2 * b * s + N_EXPERTS * MOE_TM
    mod = _adaln(c, w_ada, b_ada).reshape(n_layers, b, 6, 1, d)
    cos_t, sin_t = _rope_tables(s)
    ret_tables = _ret_tables(RET_T)
    w_in_b = w_in.astype(BF16)
    w_out_b = w_out.astype(BF16)
    n_route = N_GROUPS + N_EXPERTS
    w_route = jnp.pad(jnp.concatenate([moe_w_group, moe_w_expert], axis=-1).astype(F32),
                      ((0, 0), (0, 0), (0, LANES - n_route)))
    w_route_hi = w_route.astype(BF16)
    w_route = jnp.stack([w_route_hi, (w_route - w_route_hi.astype(F32)).astype(BF16)], axis=1)
    b_route = jnp.pad(jnp.concatenate([moe_b_group, moe_b_expert], axis=-1).astype(F32),
                      ((0, 0), (0, LANES - n_route))).reshape(n_layers, 1, LANES)
    fg = final_g.reshape(1, d).astype(F32)
    projected = _inproj(x, mod[0, :, 1], mod[0, :, 0], norm1_g[0].reshape(1, d), w_in_b[0], cos_t, sin_t, row_tile)
    for i in range(n_layers):
        _, _, g1, sh2, sc2, g2 = (mod[i, :, k] for k in range(6))
        qa, ka, va, us, qr, kr, vr, gr = projected
        y_a = _attention(qa, ka, va, _attn_bias_table(attn_rel_bias[i], ATTN_TQ))
        bm, cm, ar, ai = _ssm_params(ssm_a_re[i], ssm_a_im[i], ssm_log_dt[i], ssm_b_re[i], ssm_b_im[i],
                                     ssm_c_re[i], ssm_c_im[i])
        y_s = _ssm(us, b, bm, cm, ar, ai, ssm_d[i], ssm_w_glu[i], ssm_b_glu[i])
        y_r = _retention(qr, kr, vr, gr, ret_gn_g[i], ret_tables)
        x, h2, route = _outproj_router(x, y_a, y_s, y_r, w_out_b[i], g1, sc2, sh2,
                                       norm2_g[i].reshape(1, d), w_route[i], b_route[i], row_tile)
        expert_ids = route[:, :, 0:2].astype(jnp.int32).reshape(b * s, 2)
        dest, tile_expert, fill, meta = _slot_plan(expert_ids, n_slots)
        xs = _dispatch(h2, dest, fill, meta, n_slots)
        ys = _experts(xs, tile_expert, meta, moe_w_gate, moe_w_up, moe_w_down, i)
        if i == n_layers - 1:
            x = _combine(x, route, g2, fg, ys, dest, 256, final_norm=True)
        else:
            x, *projected = _combine_inproj(x, route, g2, ys, dest, mod[i + 1, :, 1], mod[i + 1, :, 0],
                                            norm1_g[i + 1].reshape(1, d), w_in_b[i + 1], cos_t, sin_t, row_tile)
    return x


def kernel(x, c, norm1_g, norm2_g, w_ada, b_ada, w_in, attn_rel_bias, ssm_a_re, ssm_a_im, ssm_log_dt, ssm_b_re, ssm_b_im, ssm_c_re, ssm_c_im, ssm_d, ssm_w_glu, ssm_b_glu, ret_gn_g, w_out, moe_w_group, moe_b_group, moe_w_expert, moe_b_expert, moe_w_gate, moe_w_up, moe_w_down, final_g):
    return _trunk(x, c, norm1_g, norm2_g, w_ada, b_ada, w_in, attn_rel_bias, ssm_a_re, ssm_a_im, ssm_log_dt,
                  ssm_b_re, ssm_b_im, ssm_c_re, ssm_c_im, ssm_d, ssm_w_glu, ssm_b_glu, ret_gn_g, w_out,
                  moe_w_group, moe_b_group, moe_w_expert, moe_b_expert, moe_w_gate, moe_w_up, moe_w_down,
                  final_g, row_tile=512)
```

```python
import functools
import math

import jax
import jax.numpy as jnp
import numpy as np
from jax import lax
from jax.experimental import pallas as pl
from jax.experimental.pallas import tpu as pltpu

F32 = jnp.float32
BF16 = jnp.bfloat16

D_MODEL = 1024
N_LAYERS = 4
CHUNK = 64
HEAD_DIM = 64
NORM_EPS = 1e-6

ATTN_HEADS = 8
ATTN_WIDTH = ATTN_HEADS * HEAD_DIM
LEFT_CHUNKS = 8
MAX_REL = 128
REL_TABLE = MAX_REL + CHUNK

SSM_WIDTH = 256
SSM_GROUP = 16
SSM_GROUPS = 16
SSM_STATE = 64
SSM_COMPLEX = SSM_GROUPS * SSM_STATE

RET_HEADS = 4
RET_WIDTH = RET_HEADS * HEAD_DIM
ROPE_BASE = 10000.0

N_GROUPS = 4
EXPERTS_PER_GROUP = 8
N_EXPERTS = N_GROUPS * EXPERTS_PER_GROUP
EXPERT_FF = 256

SUBLANES = 8
LANES = 128
VMEM_LIMIT = 56 * 1024 * 1024

ATTN_TQ = 256
RET_T = 256
SSM_TS = 64
MOE_TM = 512
ROW_TILES = D_MODEL // LANES
NEG_BIG = -1e30
LOG2_E = math.log2(math.e)


def _sigmoid(x):
    return 1.0 / (1.0 + jnp.exp(-x))


def _cparams(*sem):
    return pltpu.CompilerParams(dimension_semantics=sem, vmem_limit_bytes=VMEM_LIMIT)


def _adaln_body(c_ref, w_ref, b_ref, o_ref):
    c = c_ref[...]
    cond = c * _sigmoid(c)
    o_ref[...] = jnp.dot(cond, w_ref[...], preferred_element_type=F32,
                         precision=lax.Precision.HIGHEST) + b_ref[...]


def _adaln(c, w_ada, b_ada):
    n_l, d, d6 = w_ada.shape
    b = c.shape[0]
    tn = 1536
    return pl.pallas_call(
        _adaln_body,
        grid=(n_l, d6 // tn),
        in_specs=[pl.BlockSpec((b, d), lambda l, j: (0, 0)),
                  pl.BlockSpec((None, d, tn), lambda l, j: (l, 0, j)),
                  pl.BlockSpec((None, 1, tn), lambda l, j: (l, 0, j))],
        out_specs=pl.BlockSpec((None, b, tn), lambda l, j: (l, 0, j)),
        out_shape=jax.ShapeDtypeStruct((n_l, b, d6), F32),
        compiler_params=_cparams("arbitrary", "arbitrary"),
        name="adaln",
    )(c, w_ada, b_ada.reshape(n_l, 1, d6))


def _modulated_norm(x, g, sc, sh):
    ms = jnp.mean(x * x, axis=-1, keepdims=True)
    return (x * lax.rsqrt(ms + NORM_EPS) * g) * (1.0 + sc) + sh


def _inproj_body(x_ref, *refs):
    _inproj_project(x_ref[...], *refs)


def _inproj_project(x, sc_ref, sh_ref, g_ref, w_ref, cos_ref, sin_ref,
                    qa_ref, ka_ref, va_ref, us_ref, qr_ref, kr_ref, vr_ref, gr_ref, between=None):
    hb = _modulated_norm(x, g_ref[...], sc_ref[...], sh_ref[...]).astype(BF16)
    stage = iter(range(INPROJ_STAGES))

    def proj(lo, width):
        if between is not None:
            between(next(stage))
        return jnp.dot(hb, w_ref[:, lo:lo + width], preferred_element_type=F32)

    scale = HEAD_DIM ** -0.5
    a = ATTN_WIDTH
    qa_ref[...] = (proj(0, a) * (scale * LOG2_E)).astype(BF16)
    ka_ref[...] = proj(a, a).astype(BF16)
    va_ref[...] = proj(2 * a, a).astype(BF16)
    o = 3 * a
    us_ref[...] = proj(o, SSM_WIDTH)
    o += SSM_WIDTH

    cos = cos_ref[...]
    sin = sin_ref[...]
    lane = lax.broadcasted_iota(jnp.int32, cos.shape, 1)
    first_half = (lane & (HEAD_DIM // 2)) == 0

    def rotary(z):
        partner = jnp.where(first_half,
                            pltpu.roll(z, RET_WIDTH - HEAD_DIM // 2, 1),
                            pltpu.roll(z, HEAD_DIM // 2, 1))
        return z * cos + partner * sin

    r = RET_WIDTH
    qr_ref[...] = rotary(proj(o, r)).astype(BF16)
    kr_ref[...] = (rotary(proj(o + r, r)) * scale).astype(BF16)
    vr_ref[...] = proj(o + 2 * r, r).astype(BF16)
    gr_ref[...] = proj(o + 3 * r, r)


def _inproj(x, sc, sh, g, w_bf16, cos_t, sin_t, ts):
    b, s, d = x.shape
    nj = s // ts
    tok = lambda width: pl.BlockSpec((None, ts, width), lambda bi, j: (bi, j, 0))
    per_b = pl.BlockSpec((None, 1, d), lambda bi, j: (bi, 0, 0))
    rope = pl.BlockSpec((ts, RET_WIDTH), lambda bi, j: (j, 0))
    sds = lambda width, dt: jax.ShapeDtypeStruct((b, s, width), dt)
    return pl.pallas_call(
        _inproj_body,
        grid=(b, nj),
        in_specs=[tok(d), per_b, per_b,
                  pl.BlockSpec((1, d), lambda bi, j: (0, 0)),
                  pl.BlockSpec(w_bf16.shape, lambda bi, j: (0, 0)),
                  rope, rope],
        out_specs=[tok(ATTN_WIDTH), tok(ATTN_WIDTH), tok(ATTN_WIDTH),
                   pl.BlockSpec((ts, SSM_WIDTH), lambda bi, j: (j, bi)),
                   tok(RET_WIDTH), tok(RET_WIDTH), tok(RET_WIDTH), tok(RET_WIDTH)],
        out_shape=[sds(ATTN_WIDTH, BF16), sds(ATTN_WIDTH, BF16), sds(ATTN_WIDTH, BF16),
                   jax.ShapeDtypeStruct((s, b * SSM_WIDTH), F32),
                   sds(RET_WIDTH, BF16), sds(RET_WIDTH, BF16), sds(RET_WIDTH, BF16), sds(RET_WIDTH, F32)],
        compiler_params=_cparams("parallel", "arbitrary"),
        name="inproj",
    )(x, sc, sh, g, w_bf16, cos_t, sin_t)


def _attn_body(q_ref, k0_ref, k1_ref, k2_ref, v0_ref, v1_ref, v2_ref, bias_ref, o_ref):
    j = pl.program_id(1)
    tq = q_ref.shape[0]
    lane = lax.broadcasted_iota(jnp.int32, (1, LANES), 1)
    low = lane < HEAD_DIM
    ones_cols = jnp.ones((3 * tq, LANES), BF16)

    def heads(mask_start):
        if mask_start:
            col = lax.broadcasted_iota(jnp.int32, (tq, 3 * tq), 1)
            in_seq = col >= (2 - j) * tq
        for hp in range(ATTN_HEADS // 2):
            sl = slice(hp * LANES, (hp + 1) * LANES)
            q2 = q_ref[:, sl]
            k2 = jnp.concatenate([k0_ref[:, sl], k1_ref[:, sl], k2_ref[:, sl]], axis=0)
            v2 = jnp.concatenate([v0_ref[:, sl], v1_ref[:, sl], v2_ref[:, sl]], axis=0)
            v_aug = jnp.concatenate([v2, ones_cols], axis=1)
            outs = []
            for sub in range(2):
                keep = low if sub == 0 else jnp.logical_not(low)
                kz = jnp.where(keep, k2, jnp.zeros_like(k2))
                s = lax.dot_general(q2, kz, (((1,), (1,)), ((), ())), preferred_element_type=F32)
                s = s + bias_ref[2 * hp + sub]
                if mask_start:
                    s = jnp.where(in_seq, s, NEG_BIG)
                p = jnp.exp2(s - jnp.max(s, axis=-1, keepdims=True)).astype(BF16)
                r = jnp.dot(p, v_aug, preferred_element_type=F32)
                outs.append(r[:, :LANES] * (1.0 / r[:, LANES:LANES + 1]))
            o_ref[:, sl] = jnp.where(low, outs[0], outs[1]).astype(BF16)

    @pl.when(j < 2)
    def _():
        heads(True)

    @pl.when(j >= 2)
    def _():
        heads(False)


def _attn_bias_table(rel_bias, tq):
    r = np.arange(tq)[:, None]
    c = np.arange(3 * tq)[None, :]
    qc = r // CHUNK + 2 * tq // CHUNK
    kc = c // CHUNK
    band = (kc >= qc - LEFT_CHUNKS) & (kc <= qc)
    period = 4 * tq
    k = np.arange(period)
    diff = np.where(k < 3 * tq, k, k - period)
    idx = np.clip(diff - 2 * tq, -MAX_REL, CHUNK - 1) + MAX_REL
    vec = rel_bias.astype(F32)[:, idx]
    h = vec.shape[0]
    table = jnp.tile(vec, (1, tq))[:, :tq * (period - 1)].reshape(h, tq, period - 1)[:, :, :3 * tq]
    return jnp.where(jnp.asarray(band)[None], table * LOG2_E, NEG_BIG)


def _attention(qa, ka, va, bias):
    b, s, w = qa.shape
    tq = ATTN_TQ
    qspec = pl.BlockSpec((None, tq, w), lambda bi, j: (bi, j, 0))
    kspec = lambda back: pl.BlockSpec((None, tq, w), lambda bi, j: (bi, jnp.maximum(j - back, 0), 0))
    return pl.pallas_call(
        _attn_body,
        grid=(b, s // tq),
        in_specs=[qspec, kspec(2), kspec(1), kspec(0), kspec(2), kspec(1), kspec(0),
                  pl.BlockSpec(bias.shape, lambda bi, j: (0, 0, 0))],
        out_specs=qspec,
        out_shape=jax.ShapeDtypeStruct((b, s, w), BF16),
        compiler_params=_cparams("parallel", "arbitrary"),
        name="band_attn",
    )(qa, ka, ka, ka, va, va, va, bias)


def _gelu_tanh(x):
    return 0.5 * x * (1.0 + jnp.tanh(math.sqrt(2.0 / math.pi) * (x + 0.044715 * (x * x * x))))


def _ssm_body(u_ref, bm_ref, cm_ref, ar_ref, ai_ref, d_ref, wg_ref, bg_ref, y_ref,
              st_ref, bu_ref, ut_ref, yt_ref):
    nb = st_ref.shape[1]
    ts = u_ref.shape[0]
    nc = SSM_COMPLEX
    w = SSM_WIDTH
    w_ct = w // LANES

    @pl.when(pl.program_id(0) == 0)
    def _():
        st_ref[...] = jnp.zeros_like(st_ref)

    for bi in range(nb):
        for ct in range(w_ct):
            lo = bi * w + ct * LANES
            ut_ref.at[ct][pl.ds(bi, ts, stride=nb), :] = u_ref[:, lo:lo + LANES]
    u_tb = jnp.concatenate([ut_ref[ct] for ct in range(w_ct)], axis=1)
    n_ct = 2 * nc // LANES
    bu = jnp.dot(u_tb.astype(BF16), bm_ref[...], preferred_element_type=F32)
    for ct in range(n_ct):
        bu_ref[ct] = bu[:, ct * LANES:(ct + 1) * LANES]

    tiles_per_block = 4
    half = nc // LANES
    for t0 in range(0, half, tiles_per_block):
        tiles = range(t0, t0 + tiles_per_block)
        ar = [jnp.broadcast_to(ar_ref[:, ct * LANES:(ct + 1) * LANES], (nb, LANES)) for ct in tiles]
        ai = [jnp.broadcast_to(ai_ref[:, ct * LANES:(ct + 1) * LANES], (nb, LANES)) for ct in tiles]

        def step(t, carry):
            rows = pl.ds(pl.multiple_of(t * nb, nb), nb)
            out = []
            for k, ct in enumerate(tiles):
                xr, xi = carry[2 * k], carry[2 * k + 1]
                nr = ar[k] * xr - ai[k] * xi + bu_ref[ct, rows, :]
                ni = ar[k] * xi + ai[k] * xr + bu_ref[half + ct, rows, :]
                bu_ref[ct, rows, :] = nr
                bu_ref[half + ct, rows, :] = ni
                out += [nr, ni]
            return tuple(out)

        init = []
        for ct in tiles:
            init += [st_ref[0, :, ct * LANES:(ct + 1) * LANES], st_ref[1, :, ct * LANES:(ct + 1) * LANES]]
        fin = lax.fori_loop(0, ts, step, tuple(init), unroll=4)
        for k, ct in enumerate(tiles):
            st_ref[0, :, ct * LANES:(ct + 1) * LANES] = fin[2 * k]
            st_ref[1, :, ct * LANES:(ct + 1) * LANES] = fin[2 * k + 1]

    states = jnp.concatenate([bu_ref[ct] for ct in range(n_ct)], axis=1).astype(BF16)
    u_tb = jnp.concatenate([ut_ref[ct] for ct in range(w_ct)], axis=1)
    y = _gelu_tanh(jnp.dot(states, cm_ref[...], preferred_element_type=F32) + d_ref[...] * u_tb)
    z = jnp.dot(y.astype(BF16), wg_ref[...], preferred_element_type=F32) + bg_ref[...]
    y = y * _sigmoid(z)
    for ct in range(w_ct):
        yt_ref[ct] = y[:, ct * LANES:(ct + 1) * LANES]
    for bi in range(nb):
        for ct in range(w_ct):
            lo = bi * w + ct * LANES
            y_ref[:, lo:lo + LANES] = yt_ref.at[ct][pl.ds(bi, ts, stride=nb), :].astype(BF16)


def _ssm_params(a_re, a_im, log_dt, b_re, b_im, c_re, c_im):
    g, p, cg = b_re.shape
    lam = lax.complex(a_re.astype(F32), a_im.astype(F32))
    dt = jnp.exp(log_dt.astype(F32))[:, None]
    a_bar = jnp.exp(lam * dt)
    b_bar = ((a_bar - 1.0) / lam)[:, :, None] * lax.complex(b_re.astype(F32), b_im.astype(F32))
    eye = jnp.eye(g, dtype=F32)
    b_blk = lambda m: jnp.einsum('gpc,gh->gchp', m, eye).reshape(g * cg, g * p)
    bm = jnp.concatenate([b_blk(jnp.real(b_bar)), b_blk(jnp.imag(b_bar))], axis=1)
    c_blk = lambda m: jnp.einsum('gcp,gh->gphc', m.astype(F32), eye).reshape(g * p, g * cg)
    cm = jnp.concatenate([c_blk(c_re), -c_blk(c_im)], axis=0)
    ar = jnp.real(a_bar).reshape(1, g * p)
    ai = jnp.imag(a_bar).reshape(1, g * p)
    return bm.astype(BF16), cm.astype(BF16), ar, ai


def _ssm(us, nb, bm, cm, ar, ai, d_skip, w_glu, b_glu):
    s_len, wide = us.shape
    w = SSM_WIDTH
    ts = SSM_TS
    full = lambda a: pl.BlockSpec(a.shape, lambda i: (0,) * a.ndim)
    d2 = d_skip.reshape(1, w).astype(F32)
    bg2 = b_glu.reshape(1, w).astype(F32)
    wg = w_glu.astype(BF16)
    return pl.pallas_call(
        _ssm_body,
        grid=(s_len // ts,),
        in_specs=[pl.BlockSpec((ts, wide), lambda i: (i, 0)),
                  full(bm), full(cm), full(ar), full(ai), full(d2), full(wg), full(bg2)],
        out_specs=pl.BlockSpec((ts, wide), lambda i: (i, 0)),
        out_shape=jax.ShapeDtypeStruct((s_len, wide), BF16),
        scratch_shapes=[pltpu.VMEM((2, nb, SSM_COMPLEX), F32),
                        pltpu.VMEM((2 * SSM_COMPLEX // LANES, ts * nb, LANES), F32),
                        pltpu.VMEM((w // LANES, ts * nb, LANES), F32),
                        pltpu.VMEM((w // LANES, ts * nb, LANES), F32)],
        compiler_params=_cparams("arbitrary"),
        name="s5",
    )(us, bm, cm, ar, ai, d2, wg, bg2)


def _ret_body(q_ref, k_ref, v_ref, g_ref, gn_ref, dec_ref, xi_ref, zeta_ref, cd_ref, avg_ref, o_ref, st_ref):
    @pl.when(pl.program_id(1) == 0)
    def _():
        st_ref[...] = jnp.zeros_like(st_ref)

    lane = lax.broadcasted_iota(jnp.int32, (1, LANES), 1)
    low = lane < HEAD_DIM
    avg = avg_ref[...]

    def head_mean(t):
        hi = t.astype(BF16)
        lo = (t - hi.astype(F32)).astype(BF16)
        return jnp.dot(hi, avg, preferred_element_type=F32) + jnp.dot(lo, avg, preferred_element_type=F32)

    for hp in range(RET_HEADS // 2):
        sl = slice(hp * LANES, (hp + 1) * LANES)
        q2 = q_ref[:, sl]
        k2 = k_ref[:, sl]
        v2 = v_ref[:, sl]
        parts = []
        for sub in range(2):
            keep = low if sub == 0 else jnp.logical_not(low)
            k_one = jnp.where(keep, k2, jnp.zeros_like(k2))
            a = lax.dot_general(q2, k_one, (((1,), (1,)), ((), ())), preferred_element_type=F32)
            a = a * dec_ref[2 * hp + sub]
            parts.append(jnp.dot(a.astype(BF16), v2, preferred_element_type=F32))
        inner = jnp.where(low, parts[0], parts[1])
        st = st_ref[hp]
        qx = (q2.astype(F32) * xi_ref[hp]).astype(BF16)
        y = inner + jnp.dot(qx, st.astype(BF16), preferred_element_type=F32)
        kz = (k2.astype(F32) * zeta_ref[hp]).astype(BF16)
        kv = lax.dot_general(kz, v2, (((0,), (0,)), ((), ())), preferred_element_type=F32)
        blocks = cd_ref[hp]
        st_ref[hp] = blocks * st + jnp.where(blocks > 0.0, kv, 0.0)
        yc = y - head_mean(y)
        var = head_mean(yc * yc)
        yn = yc * lax.rsqrt(var + NORM_EPS) * gn_ref[:, sl]
        g = g_ref[:, sl]
        o_ref[:, sl] = (g * _sigmoid(g) * yn).astype(BF16)


def _ret_tables(t_len):
    n_pairs = RET_HEADS // 2
    log_gamma = jnp.log1p(-jnp.exp2(-5.0 - jnp.arange(RET_HEADS, dtype=F32)))
    t = jnp.arange(t_len, dtype=F32)
    diff = t[:, None] - t[None, :]
    dec = jnp.where(diff >= 0, jnp.exp(log_gamma[:, None, None] * jnp.maximum(diff, 0.0)), 0.0)
    per_lane = lambda m: jnp.repeat(m.reshape(n_pairs, 2, -1), HEAD_DIM, axis=1).transpose(0, 2, 1)
    xi = per_lane(jnp.exp(log_gamma[:, None] * (t + 1.0)))
    zeta = per_lane(jnp.exp(log_gamma[:, None] * (t_len - 1 - t)))
    lane_head = jnp.arange(LANES) // HEAD_DIM
    same = (lane_head[:, None] == lane_head[None, :]).astype(F32)
    cd_lane = jnp.repeat(jnp.exp(log_gamma * t_len).reshape(n_pairs, 2), HEAD_DIM, axis=1)
    cd = cd_lane[:, :, None] * same[None]
    avg = (same / HEAD_DIM).astype(BF16)
    return dec, xi, zeta, cd, avg


def _retention(qr, kr, vr, gr, gn_g, tables):
    b, s, w = qr.shape
    t_len = RET_T
    dec, xi, zeta, cd, avg = tables
    tok = pl.BlockSpec((None, t_len, w), lambda bi, j: (bi, j, 0))
    full = lambda a: pl.BlockSpec(a.shape, lambda bi, j: (0,) * a.ndim)
    gn2 = gn_g.reshape(1, w).astype(F32)
    return pl.pallas_call(
        _ret_body,
        grid=(b, s // t_len),
        in_specs=[tok, tok, tok, tok, full(gn2), full(dec), full(xi), full(zeta), full(cd), full(avg)],
        out_specs=tok,
        out_shape=jax.ShapeDtypeStruct((b, s, w), BF16),
        scratch_shapes=[pltpu.VMEM((RET_HEADS // 2, LANES, LANES), F32)],
        compiler_params=_cparams("parallel", "arbitrary"),
        name="retention",
    )(qr, kr, vr, gr, gn2, dec, xi, zeta, cd, avg)


def _store_row_tiled(ref, val):
    rows = val.shape[0]
    for t in range(ROW_TILES):
        ref[pl.ds(t, rows, stride=ROW_TILES), :] = val[:, t * LANES:(t + 1) * LANES]


def _load_row_tiled(ref, first, rows):
    return jnp.concatenate(
        [ref[pl.ds(first * ROW_TILES + t, rows, stride=ROW_TILES), :] for t in range(ROW_TILES)], axis=1)


def _outproj_body(x_ref, ya_ref, ys_ref, yr_ref, w_ref, g1_ref, sc_ref, sh_ref, ng_ref, wr_ref, br_ref,
                  xo_ref, h2_ref, route_ref):
    a = ATTN_WIDTH
    mixed = jnp.dot(ya_ref[...], w_ref[0:a, :], preferred_element_type=F32)
    mixed += jnp.dot(ys_ref[...], w_ref[a:a + SSM_WIDTH, :], preferred_element_type=F32)
    mixed += jnp.dot(yr_ref[...], w_ref[a + SSM_WIDTH:, :], preferred_element_type=F32)
    x = x_ref[...] + g1_ref[...] * mixed
    xo_ref[...] = x
    h2 = _modulated_norm(x, ng_ref[...], sc_ref[...], sh_ref[...])
    _store_row_tiled(h2_ref, h2)

    h_hi = h2.astype(BF16)
    h_lo = (h2 - h_hi.astype(F32)).astype(BF16)
    r_hi = jnp.dot(h_hi, wr_ref[...], preferred_element_type=F32)
    r_lo = jnp.dot(h_lo, wr_ref[...], preferred_element_type=F32)
    logits = (r_hi[:, :LANES] + ((r_hi[:, LANES:] + r_lo[:, :LANES]) + r_lo[:, LANES:])
              + br_ref[...])
    lane = lax.broadcasted_iota(jnp.int32, logits.shape, 1).astype(F32)
    big = float(LANES)

    def first_argmax(vals):
        m = jnp.max(vals, axis=-1, keepdims=True)
        return m, jnp.min(jnp.where(vals == m, lane, big), axis=-1, keepdims=True)

    gl = jnp.where(lane < N_GROUPS, logits, -jnp.inf)
    gmax, gidx = first_argmax(gl)
    group_w = 1.0 / jnp.sum(jnp.exp(gl - gmax), axis=-1, keepdims=True)
    e_lo = N_GROUPS + gidx * EXPERTS_PER_GROUP
    el = jnp.where((lane >= e_lo) & (lane < e_lo + EXPERTS_PER_GROUP), logits, -jnp.inf)
    v1, i1 = first_argmax(el)
    v2, i2 = first_argmax(jnp.where(lane == i1, -jnp.inf, el))
    t2 = jnp.exp(v2 - v1)
    w1 = group_w / (1.0 + t2)
    w2 = group_w * t2 / (1.0 + t2)
    route = jnp.where(lane == 0, i1 - N_GROUPS,
                      jnp.where(lane == 1, i2 - N_GROUPS,
                                jnp.where(lane == 2, w1, jnp.where(lane == 3, w2, 0.0))))
    route_ref[...] = route


def _outproj_router(x, ya, ys_tm, yr, w_out_bf16, g1, sc2, sh2, ng, w_route, b_route, ts):
    b, s, d = x.shape
    tok = lambda width: pl.BlockSpec((None, ts, width), lambda bi, j: (bi, j, 0))
    per_b = pl.BlockSpec((None, 1, d), lambda bi, j: (bi, 0, 0))
    full = lambda a: pl.BlockSpec(a.shape, lambda bi, j: (0,) * a.ndim)
    nj = s // ts
    return pl.pallas_call(
        _outproj_body,
        grid=(b, nj),
        in_specs=[tok(d), tok(ATTN_WIDTH),
                  pl.BlockSpec((ts, SSM_WIDTH), lambda bi, j: (j, bi)),
                  tok(RET_WIDTH), full(w_out_bf16), per_b, per_b, per_b, full(ng),
                  full(w_route), full(b_route)],
        out_specs=[tok(d),
                   pl.BlockSpec((ts * ROW_TILES, LANES), lambda bi, j: (bi * nj + j, 0)),
                   pl.BlockSpec((None, ts, LANES), lambda bi, j: (bi, j, 0))],
        out_shape=[jax.ShapeDtypeStruct((b, s, d), F32),
                   jax.ShapeDtypeStruct((b * s * ROW_TILES, LANES), F32),
                   jax.ShapeDtypeStruct((b, s, LANES), F32)],
        compiler_params=_cparams("parallel", "arbitrary"),
        name="outproj_router",
    )(x, ya, ys_tm, yr, w_out_bf16, g1, sc2, sh2, ng, w_route, b_route)


def _slot_plan(expert_ids, n_slots):
    tm = MOE_TM
    experts = jnp.arange(N_EXPERTS, dtype=jnp.int32)[None, :]
    first = (expert_ids[:, 0:1] == experts).astype(jnp.int32)
    second = (expert_ids[:, 1:2] == experts).astype(jnp.int32)
    both = first + second
    csum = jnp.cumsum(both, axis=0)
    counts = csum[-1]
    padded = ((counts + tm - 1) // tm) * tm
    ends = jnp.cumsum(padded)
    starts = ends - padded
    base = starts[None, :] + (csum - both)
    dest = jnp.stack([jnp.sum(first * base, axis=1), jnp.sum(second * (base + first), axis=1)],
                     axis=1).reshape(-1).astype(jnp.int32)
    n_tiles = n_slots // tm
    tile_row = jnp.arange(n_tiles, dtype=jnp.int32) * tm
    tile_expert = jnp.minimum(jnp.sum((tile_row[:, None] >= ends[None, :]).astype(jnp.int32), axis=1),
                              N_EXPERTS - 1).astype(jnp.int32)
    used_tiles = (ends[-1] // tm).astype(jnp.int32)
    group_last = jnp.where(padded > 0, ends - tm, n_slots)
    tail = jnp.where(tile_row >= ends[-1], tile_row, n_slots)
    fill = jnp.sort(jnp.concatenate([group_last, tail]).astype(jnp.int32))
    n_fill = jnp.sum(fill < n_slots).astype(jnp.int32)
    meta = jnp.stack([n_fill, used_tiles]).astype(jnp.int32)
    return dest, tile_expert, fill, meta


INPROJ_STAGES = 8
DMA_UNROLL = 8


def _dispatch_body(dest_ref, fill_ref, meta_ref, h2_ref, xs_hbm, zero_ref, sem):
    step = pl.program_id(0)
    tile_rows = MOE_TM * ROW_TILES
    tokens = h2_ref.shape[0] // ROW_TILES

    def fill_copy(i):
        row = pl.multiple_of(fill_ref[i] * ROW_TILES, tile_rows)
        return pltpu.make_async_copy(zero_ref, xs_hbm.at[pl.ds(row, tile_rows), :], sem)

    @pl.when(step == 0)
    def _():
        zero_ref[...] = jnp.zeros_like(zero_ref)
        n_fill = meta_ref[0]
        lax.fori_loop(0, n_fill, lambda i, c: (fill_copy(i).start(), c)[1], 0)
        lax.fori_loop(0, n_fill, lambda i, c: (fill_copy(i).wait(), c)[1], 0)

    base = step * (2 * tokens)

    def row_copy(i, choice):
        src = pl.multiple_of(i * ROW_TILES, ROW_TILES)
        dst = pl.multiple_of(dest_ref[base + 2 * i + choice] * ROW_TILES, ROW_TILES)
        return pltpu.make_async_copy(h2_ref.at[pl.ds(src, ROW_TILES), :],
                                     xs_hbm.at[pl.ds(dst, ROW_TILES), :], sem)

    def start(i, c):
        row_copy(i, 0).start(priority=0)
        row_copy(i, 1).start(priority=1)
        return c

    def wait(i, c):
        row_copy(i, 0).wait()
        row_copy(i, 1).wait()
        return c

    lax.fori_loop(0, tokens, start, 0, unroll=DMA_UNROLL)
    lax.fori_loop(0, tokens, wait, 0, unroll=DMA_UNROLL)


def _dispatch(h2_tiled, dest, fill, meta, n_slots):
    tokens = 512
    rows = tokens * ROW_TILES
    return pl.pallas_call(
        _dispatch_body,
        grid_spec=pltpu.PrefetchScalarGridSpec(
            num_scalar_prefetch=3,
            grid=(h2_tiled.shape[0] // rows,),
            in_specs=[pl.BlockSpec((rows, LANES), lambda i, d, f, m: (i, 0))],
            out_specs=pl.BlockSpec(memory_space=pl.ANY),
            scratch_shapes=[pltpu.VMEM((MOE_TM * ROW_TILES, LANES), F32), pltpu.SemaphoreType.DMA],
        ),
        out_shape=jax.ShapeDtypeStruct((n_slots * ROW_TILES, LANES), F32),
        compiler_params=pltpu.CompilerParams(dimension_semantics=("arbitrary",), has_side_effects=True,
                                             vmem_limit_bytes=VMEM_LIMIT),
        name="moe_dispatch",
    )(dest, fill, meta, h2_tiled)


def _experts_body(te_ref, meta_ref, xs_ref, wg_ref, wu_ref, wd_ref, ys_ref):
    tm = MOE_TM

    @pl.when(pl.program_id(0) < meta_ref[1])
    def _():
        x = _load_row_tiled(xs_ref, 0, tm).astype(BF16)
        hg = jnp.dot(x, wg_ref[...].astype(BF16), preferred_element_type=F32)
        hu = jnp.dot(x, wu_ref[...].astype(BF16), preferred_element_type=F32)
        act = (hg * _sigmoid(hg) * hu).astype(BF16)
        _store_row_tiled(ys_ref, jnp.dot(act, wd_ref[...].astype(BF16), preferred_element_type=F32))

    @pl.when(pl.program_id(0) >= meta_ref[1])
    def _():
        ys_ref[...] = jnp.zeros_like(ys_ref)


def _experts(xs, tile_expert, meta, w_gate, w_up, w_down, layer):
    n_tiles = tile_expert.shape[0]
    rows = MOE_TM * ROW_TILES
    d, f = w_gate.shape[2], w_gate.shape[3]
    return pl.pallas_call(
        _experts_body,
        grid_spec=pltpu.PrefetchScalarGridSpec(
            num_scalar_prefetch=2,
            grid=(n_tiles,),
            in_specs=[pl.BlockSpec((rows, LANES), lambda i, te, mt: (i, 0)),
                      pl.BlockSpec((None, None, d, f), lambda i, te, mt: (layer, te[i], 0, 0)),
                      pl.BlockSpec((None, None, d, f), lambda i, te, mt: (layer, te[i], 0, 0)),
                      pl.BlockSpec((None, None, f, d), lambda i, te, mt: (layer, te[i], 0, 0))],
            out_specs=pl.BlockSpec((rows, LANES), lambda i, te, mt: (i, 0)),
        ),
        out_shape=jax.ShapeDtypeStruct(xs.shape, F32),
        compiler_params=_cparams("arbitrary"),
        name="moe_experts",
    )(tile_expert, meta, xs, w_gate, w_up, w_down)


def _combine_body(dest_ref, x_ref, route_ref, g2_ref, fg_ref, ys_hbm, o_ref, buf_ref, sem, *, final_norm):
    tm = x_ref.shape[0]
    base = (pl.program_id(0) * pl.num_programs(1) + pl.program_id(1)) * tm

    def row_copy(i, choice):
        src = pl.multiple_of(dest_ref[(base + i) * 2 + choice] * ROW_TILES, ROW_TILES)
        dst = pl.multiple_of((choice * tm + i) * ROW_TILES, ROW_TILES)
        return pltpu.make_async_copy(ys_hbm.at[pl.ds(src, ROW_TILES), :],
                                     buf_ref.at[pl.ds(dst, ROW_TILES), :], sem)

    def start(i, c):
        row_copy(i, 0).start(priority=0)
        row_copy(i, 1).start(priority=1)
        return c

    def wait(i, c):
        row_copy(i, 0).wait()
        row_copy(i, 1).wait()
        return c

    lax.fori_loop(0, tm, start, 0, unroll=DMA_UNROLL)
    lax.fori_loop(0, tm, wait, 0, unroll=DMA_UNROLL)
    route = route_ref[...]
    y = route[:, 2:3] * _load_row_tiled(buf_ref, 0, tm) + route[:, 3:4] * _load_row_tiled(buf_ref, tm, tm)
    x = x_ref[...] + g2_ref[...] * y
    if final_norm:
        ms = jnp.mean(x * x, axis=-1, keepdims=True)
        x = x * lax.rsqrt(ms + NORM_EPS) * fg_ref[...]
    o_ref[...] = x


def _combine(x, route, g2, final_g, ys, dest, tm, final_norm):
    b, s, d = x.shape
    tok = lambda width: pl.BlockSpec((None, tm, width), lambda bi, j, dst: (bi, j, 0))
    return pl.pallas_call(
        functools.partial(_combine_body, final_norm=final_norm),
        grid_spec=pltpu.PrefetchScalarGridSpec(
            num_scalar_prefetch=1,
            grid=(b, s // tm),
            in_specs=[tok(d), tok(LANES),
                      pl.BlockSpec((None, 1, d), lambda bi, j, dst: (bi, 0, 0)),
                      pl.BlockSpec((1, d), lambda bi, j, dst: (0, 0)),
                      pl.BlockSpec(memory_space=pl.ANY)],
            out_specs=tok(d),
            scratch_shapes=[pltpu.VMEM((2 * tm * ROW_TILES, LANES), F32), pltpu.SemaphoreType.DMA],
        ),
        out_shape=jax.ShapeDtypeStruct((b, s, d), F32),
        compiler_params=_cparams("arbitrary", "arbitrary"),
        name="moe_combine",
    )(dest, x, route, g2, final_g, ys)


def _combine_inproj_body(dest_ref, x_ref, route_ref, g2_ref, ys_hbm, sc_ref, sh_ref, g_ref, w_ref, cos_ref,
                         sin_ref, xo_ref, qa_ref, ka_ref, va_ref, us_ref, qr_ref, kr_ref, vr_ref, gr_ref,
                         buf_ref, sems):
    tm = x_ref.shape[0]
    n_tiles = pl.num_programs(0) * pl.num_programs(1)
    tile = pl.program_id(0) * pl.num_programs(1) + pl.program_id(1)
    slot = tile % 2

    def row_copy(t, s, i, choice):
        src = pl.multiple_of(dest_ref[(t * tm + i) * 2 + choice] * ROW_TILES, ROW_TILES)
        dst = pl.multiple_of(((2 * s + choice) * tm + i) * ROW_TILES, ROW_TILES)
        return pltpu.make_async_copy(ys_hbm.at[pl.ds(src, ROW_TILES), :],
                                     buf_ref.at[pl.ds(dst, ROW_TILES), :], sems.at[s])

    def start_rows(t, s, lo, hi):
        def start(i, c):
            row_copy(t, s, i, 0).start(priority=0)
            row_copy(t, s, i, 1).start(priority=1)
            return c
        lax.fori_loop(lo, hi, start, 0, unroll=DMA_UNROLL)

    @pl.when(tile == 0)
    def _():
        start_rows(tile, slot, 0, tm)

    chunk = tm // INPROJ_STAGES
    nxt = jnp.minimum(tile + 1, n_tiles - 1)

    def prefetch_chunk(k):
        for i in range(k * chunk, (k + 1) * chunk):
            row_copy(nxt, 1 - slot, i, 0).start(priority=0)
            row_copy(nxt, 1 - slot, i, 1).start(priority=1)

    def wait_tile(t, s):
        def wait(i, c):
            row_copy(t, s, i, 0).wait()
            row_copy(t, s, i, 1).wait()
            return c
        lax.fori_loop(0, tm, wait, 0, unroll=DMA_UNROLL)

    wait_tile(tile, slot)
    route = route_ref[...]
    first = 2 * slot * tm
    y = (route[:, 2:3] * _load_row_tiled(buf_ref, first, tm)
         + route[:, 3:4] * _load_row_tiled(buf_ref, first + tm, tm))
    x = x_ref[...] + g2_ref[...] * y
    xo_ref[...] = x
    _inproj_project(x, sc_ref, sh_ref, g_ref, w_ref, cos_ref, sin_ref,
                    qa_ref, ka_ref, va_ref, us_ref, qr_ref, kr_ref, vr_ref, gr_ref, between=prefetch_chunk)

    @pl.when(tile == n_tiles - 1)
    def _():
        wait_tile(nxt, 1 - slot)


def _combine_inproj(x, route, g2, ys, dest, sc, sh, g, w_bf16, cos_t, sin_t, ts):
    b, s, d = x.shape
    tok = lambda width: pl.BlockSpec((None, ts, width), lambda bi, j, dst: (bi, j, 0))
    per_b = pl.BlockSpec((None, 1, d), lambda bi, j, dst: (bi, 0, 0))
    rope = pl.BlockSpec((ts, RET_WIDTH), lambda bi, j, dst: (j, 0))
    sds = lambda width, dt: jax.ShapeDtypeStruct((b, s, width), dt)
    return pl.pallas_call(
        _combine_inproj_body,
        grid_spec=pltpu.PrefetchScalarGridSpec(
            num_scalar_prefetch=1,
            grid=(b, s // ts),
            in_specs=[tok(d), tok(LANES), per_b, pl.BlockSpec(memory_space=pl.ANY), per_b, per_b,
                      pl.BlockSpec((1, d), lambda bi, j, dst: (0, 0)),
                      pl.BlockSpec(w_bf16.shape, lambda bi, j, dst: (0, 0)),
                      rope, rope],
            out_specs=[tok(d), tok(ATTN_WIDTH), tok(ATTN_WIDTH), tok(ATTN_WIDTH),
                       pl.BlockSpec((ts, SSM_WIDTH), lambda bi, j, dst: (j, bi)),
                       tok(RET_WIDTH), tok(RET_WIDTH), tok(RET_WIDTH), tok(RET_WIDTH)],
            scratch_shapes=[pltpu.VMEM((2 * 2 * ts * ROW_TILES, LANES), F32), pltpu.SemaphoreType.DMA((2,))],
        ),
        out_shape=[sds(d, F32), sds(ATTN_WIDTH, BF16), sds(ATTN_WIDTH, BF16), sds(ATTN_WIDTH, BF16),
                   jax.ShapeDtypeStruct((s, b * SSM_WIDTH), F32),
                   sds(RET_WIDTH, BF16), sds(RET_WIDTH, BF16), sds(RET_WIDTH, BF16), sds(RET_WIDTH, F32)],
        compiler_params=_cparams("arbitrary", "arbitrary"),
        name="combine_inproj",
    )(dest, x, route, g2, ys, sc, sh, g, w_bf16, cos_t, sin_t)


def _rope_tables(s):
    half = HEAD_DIM // 2
    inv_freq = ROPE_BASE ** (-jnp.arange(half, dtype=F32) / half)
    ang = jnp.arange(s, dtype=F32)[:, None] * inv_freq[None, :]
    cos = jnp.tile(jnp.cos(ang), (1, 2 * RET_HEADS))
    sin = jnp.sin(ang)
    sin = jnp.tile(jnp.concatenate([-sin, sin], axis=1), (1, RET_HEADS))
    return cos, sin


def _trunk(x, c, norm1_g, norm2_g, w_ada, b_ada, w_in, attn_rel_bias, ssm_a_re, ssm_a_im, ssm_log_dt,
           ssm_b_re, ssm_b_im, ssm_c_re, ssm_c_im, ssm_d, ssm_w_glu, ssm_b_glu, ret_gn_g, w_out,
           moe_w_group, moe_b_group, moe_w_expert, moe_b_expert, moe_w_gate, moe_w_up, moe_w_down, final_g,
           *, row_tile):
    b, s, d = x.shape
    n_layers = w_in.shape[0]
    n_slots = 2 * b * s + N_EXPERTS * MOE_TM
    mod = _adaln(c, w_ada, b_ada).reshape(n_layers, b, 6, 1, d)
    cos_t, sin_t = _rope_tables(s)
    ret_tables = _ret_tables(RET_T)
    w_in_b = w_in.astype(BF16)
    w_out_b = w_out.astype(BF16)
    n_route = N_GROUPS + N_EXPERTS
    w_route = jnp.pad(jnp.concatenate([moe_w_group, moe_w_expert], axis=-1).astype(F32),
                      ((0, 0), (0, 0), (0, LANES - n_route)))
    w_route_hi = w_route.astype(BF16)
    w_route = jnp.concatenate([w_route_hi, (w_route - w_route_hi.astype(F32)).astype(BF16)], axis=-1)
    b_route = jnp.pad(jnp.concatenate([moe_b_group, moe_b_expert], axis=-1).astype(F32),
                      ((0, 0), (0, LANES - n_route))).reshape(n_layers, 1, LANES)
    fg = final_g.reshape(1, d).astype(F32)
    projected = _inproj(x, mod[0, :, 1], mod[0, :, 0], norm1_g[0].reshape(1, d), w_in_b[0], cos_t, sin_t, row_tile)
    for i in range(n_layers):
        _, _, g1, sh2, sc2, g2 = (mod[i, :, k] for k in range(6))
        qa, ka, va, us, qr, kr, vr, gr = projected
        y_a = _attention(qa, ka, va, _attn_bias_table(attn_rel_bias[i], ATTN_TQ))
        bm, cm, ar, ai = _ssm_params(ssm_a_re[i], ssm_a_im[i], ssm_log_dt[i], ssm_b_re[i], ssm_b_im[i],
                                     ssm_c_re[i], ssm_c_im[i])
        y_s = _ssm(us, b, bm, cm, ar, ai, ssm_d[i], ssm_w_glu[i], ssm_b_glu[i])
        y_r = _retention(qr, kr, vr, gr, ret_gn_g[i], ret_tables)
        x, h2, route = _outproj_router(x, y_a, y_s, y_r, w_out_b[i], g1, sc2, sh2,
                                       norm2_g[i].reshape(1, d), w_route[i], b_route[i], row_tile)
        expert_ids = route[:, :, 0:2].astype(jnp.int32).reshape(b * s, 2)
        dest, tile_expert, fill, meta = _slot_plan(expert_ids, n_slots)
        xs = _dispatch(h2, dest, fill, meta, n_slots)
        ys = _experts(xs, tile_expert, meta, moe_w_gate, moe_w_up, moe_w_down, i)
        if i == n_layers - 1:
            x = _combine(x, route, g2, fg, ys, dest, 256, final_norm=True)
        else:
            x, *projected = _combine_inproj(x, route, g2, ys, dest, mod[i + 1, :, 1], mod[i + 1, :, 0],
                                            norm1_g[i + 1].reshape(1, d), w_in_b[i + 1], cos_t, sin_t, row_tile)
    return x


def kernel(x, c, norm1_g, norm2_g, w_ada, b_ada, w_in, attn_rel_bias, ssm_a_re, ssm_a_im, ssm_log_dt, ssm_b_re, ssm_b_im, ssm_c_re, ssm_c_im, ssm_d, ssm_w_glu, ssm_b_glu, ret_gn_g, w_out, moe_w_group, moe_b_group, moe_w_expert, moe_b_expert, moe_w_gate, moe_w_up, moe_w_down, final_g):
    return _trunk(x, c, norm1_g, norm2_g, w_ada, b_ada, w_in, attn_rel_bias, ssm_a_re, ssm_a_im, ssm_log_dt,
                  ssm_b_re, ssm_b_im, ssm_c_re, ssm_c_im, ssm_d, ssm_w_glu, ssm_b_glu, ret_gn_g, w_out,
                  moe_w_group, moe_b_group, moe_w_expert, moe_b_expert, moe_w_gate, moe_w_up, moe_w_down,
                  final_g, row_tile=512)
```

```python
import functools
import math

import jax
import jax.numpy as jnp
import numpy as np
from jax import lax
from jax.experimental import pallas as pl
from jax.experimental.pallas import tpu as pltpu

F32 = jnp.float32
BF16 = jnp.bfloat16

D_MODEL = 1024
N_LAYERS = 4
CHUNK = 64
HEAD_DIM = 64
NORM_EPS = 1e-6

ATTN_HEADS = 8
ATTN_WIDTH = ATTN_HEADS * HEAD_DIM
LEFT_CHUNKS = 8
MAX_REL = 128
REL_TABLE = MAX_REL + CHUNK

SSM_WIDTH = 256
SSM_GROUP = 16
SSM_GROUPS = 16
SSM_STATE = 64
SSM_COMPLEX = SSM_GROUPS * SSM_STATE

RET_HEADS = 4
RET_WIDTH = RET_HEADS * HEAD_DIM
ROPE_BASE = 10000.0

N_GROUPS = 4
EXPERTS_PER_GROUP = 8
N_EXPERTS = N_GROUPS * EXPERTS_PER_GROUP
EXPERT_FF = 256

SUBLANES = 8
LANES = 128
VMEM_LIMIT = 56 * 1024 * 1024

ATTN_TQ = 256
RET_T = 256
SSM_TS = 64
MOE_TM = 512
ROW_TILES = D_MODEL // LANES
NEG_BIG = -1e30
LOG2_E = math.log2(math.e)


def _sigmoid(x):
    return 1.0 / (1.0 + jnp.exp(-x))


def _cparams(*sem):
    return pltpu.CompilerParams(dimension_semantics=sem, vmem_limit_bytes=VMEM_LIMIT)


def _adaln_body(c_ref, w_ref, b_ref, o_ref):
    c = c_ref[...]
    cond = c * _sigmoid(c)
    o_ref[...] = jnp.dot(cond, w_ref[...], preferred_element_type=F32,
                         precision=lax.Precision.HIGHEST) + b_ref[...]


def _adaln(c, w_ada, b_ada):
    n_l, d, d6 = w_ada.shape
    b = c.shape[0]
    tn = 1536
    return pl.pallas_call(
        _adaln_body,
        grid=(n_l, d6 // tn),
        in_specs=[pl.BlockSpec((b, d), lambda l, j: (0, 0)),
                  pl.BlockSpec((None, d, tn), lambda l, j: (l, 0, j)),
                  pl.BlockSpec((None, 1, tn), lambda l, j: (l, 0, j))],
        out_specs=pl.BlockSpec((None, b, tn), lambda l, j: (l, 0, j)),
        out_shape=jax.ShapeDtypeStruct((n_l, b, d6), F32),
        compiler_params=_cparams("arbitrary", "arbitrary"),
        name="adaln",
    )(c, w_ada, b_ada.reshape(n_l, 1, d6))


def _modulated_norm(x, g, sc, sh):
    ms = jnp.mean(x * x, axis=-1, keepdims=True)
    return (x * lax.rsqrt(ms + NORM_EPS) * g) * (1.0 + sc) + sh


def _inproj_body(x_ref, *refs):
    _inproj_project(x_ref[...], *refs)


def _inproj_project(x, sc_ref, sh_ref, g_ref, w_ref, cos_ref, sin_ref,
                    qa_ref, ka_ref, va_ref, us_ref, qr_ref, kr_ref, vr_ref, gr_ref, between=None):
    hb = _modulated_norm(x, g_ref[...], sc_ref[...], sh_ref[...]).astype(BF16)
    stage = iter(range(INPROJ_STAGES))

    def proj(lo, width):
        if between is not None:
            between(next(stage))
        return jnp.dot(hb, w_ref[:, lo:lo + width], preferred_element_type=F32)

    scale = HEAD_DIM ** -0.5
    a = ATTN_WIDTH
    qa_ref[...] = (proj(0, a) * (scale * LOG2_E)).astype(BF16)
    ka_ref[...] = proj(a, a).astype(BF16)
    va_ref[...] = proj(2 * a, a).astype(BF16)
    o = 3 * a
    us_ref[...] = proj(o, SSM_WIDTH)
    o += SSM_WIDTH

    cos = cos_ref[...]
    sin = sin_ref[...]
    lane = lax.broadcasted_iota(jnp.int32, cos.shape, 1)
    first_half = (lane & (HEAD_DIM // 2)) == 0

    def rotary(z):
        partner = jnp.where(first_half,
                            pltpu.roll(z, RET_WIDTH - HEAD_DIM // 2, 1),
                            pltpu.roll(z, HEAD_DIM // 2, 1))
        return z * cos + partner * sin

    r = RET_WIDTH
    qr_ref[...] = rotary(proj(o, r)).astype(BF16)
    kr_ref[...] = (rotary(proj(o + r, r)) * scale).astype(BF16)
    vr_ref[...] = proj(o + 2 * r, r).astype(BF16)
    gr_ref[...] = proj(o + 3 * r, r)


def _inproj(x, sc, sh, g, w_bf16, cos_t, sin_t, ts):
    b, s, d = x.shape
    nj = s // ts
    tok = lambda width: pl.BlockSpec((None, ts, width), lambda bi, j: (bi, j, 0))
    per_b = pl.BlockSpec((None, 1, d), lambda bi, j: (bi, 0, 0))
    rope = pl.BlockSpec((ts, RET_WIDTH), lambda bi, j: (j, 0))
    sds = lambda width, dt: jax.ShapeDtypeStruct((b, s, width), dt)
    return pl.pallas_call(
        _inproj_body,
        grid=(b, nj),
        in_specs=[tok(d), per_b, per_b,
                  pl.BlockSpec((1, d), lambda bi, j: (0, 0)),
                  pl.BlockSpec(w_bf16.shape, lambda bi, j: (0, 0)),
                  rope, rope],
        out_specs=[tok(ATTN_WIDTH), tok(ATTN_WIDTH), tok(ATTN_WIDTH),
                   pl.BlockSpec((ts, SSM_WIDTH), lambda bi, j: (j, bi)),
                   tok(RET_WIDTH), tok(RET_WIDTH), tok(RET_WIDTH), tok(RET_WIDTH)],
        out_shape=[sds(ATTN_WIDTH, BF16), sds(ATTN_WIDTH, BF16), sds(ATTN_WIDTH, BF16),
                   jax.ShapeDtypeStruct((s, b * SSM_WIDTH), F32),
                   sds(RET_WIDTH, BF16), sds(RET_WIDTH, BF16), sds(RET_WIDTH, BF16), sds(RET_WIDTH, F32)],
        compiler_params=_cparams("parallel", "arbitrary"),
        name="inproj",
    )(x, sc, sh, g, w_bf16, cos_t, sin_t)


def _attn_body(q_ref, k0_ref, k1_ref, k2_ref, v0_ref, v1_ref, v2_ref, bias_ref, o_ref):
    j = pl.program_id(1)
    tq = q_ref.shape[0]
    lane = lax.broadcasted_iota(jnp.int32, (1, LANES), 1)
    low = lane < HEAD_DIM
    ones_cols = jnp.ones((3 * tq, LANES), BF16)

    def heads(mask_start):
        if mask_start:
            col = lax.broadcasted_iota(jnp.int32, (tq, 3 * tq), 1)
            in_seq = col >= (2 - j) * tq
        for hp in range(ATTN_HEADS // 2):
            sl = slice(hp * LANES, (hp + 1) * LANES)
            q2 = q_ref[:, sl]
            k2 = jnp.concatenate([k0_ref[:, sl], k1_ref[:, sl], k2_ref[:, sl]], axis=0)
            v2 = jnp.concatenate([v0_ref[:, sl], v1_ref[:, sl], v2_ref[:, sl]], axis=0)
            v_aug = jnp.concatenate([v2, ones_cols], axis=1)
            outs = []
            for sub in range(2):
                keep = low if sub == 0 else jnp.logical_not(low)
                kz = jnp.where(keep, k2, jnp.zeros_like(k2))
                s = lax.dot_general(q2, kz, (((1,), (1,)), ((), ())), preferred_element_type=F32)
                s = s + bias_ref[2 * hp + sub]
                if mask_start:
                    s = jnp.where(in_seq, s, NEG_BIG)
                p = jnp.exp2(s - jnp.max(s, axis=-1, keepdims=True)).astype(BF16)
                r = jnp.dot(p, v_aug, preferred_element_type=F32)
                outs.append(r[:, :LANES] * (1.0 / r[:, LANES:LANES + 1]))
            o_ref[:, sl] = jnp.where(low, outs[0], outs[1]).astype(BF16)

    @pl.when(j < 2)
    def _():
        heads(True)

    @pl.when(j >= 2)
    def _():
        heads(False)


def _attn_bias_table(rel_bias, tq):
    r = np.arange(tq)[:, None]
    c = np.arange(3 * tq)[None, :]
    qc = r // CHUNK + 2 * tq // CHUNK
    kc = c // CHUNK
    band = (kc >= qc - LEFT_CHUNKS) & (kc <= qc)
    period = 4 * tq
    k = np.arange(period)
    diff = np.where(k < 3 * tq, k, k - period)
    idx = np.clip(diff - 2 * tq, -MAX_REL, CHUNK - 1) + MAX_REL
    vec = rel_bias.astype(F32)[:, idx]
    h = vec.shape[0]
    table = jnp.tile(vec, (1, tq))[:, :tq * (period - 1)].reshape(h, tq, period - 1)[:, :, :3 * tq]
    return jnp.where(jnp.asarray(band)[None], table * LOG2_E, NEG_BIG)


def _attention(qa, ka, va, bias):
    b, s, w = qa.shape
    tq = ATTN_TQ
    qspec = pl.BlockSpec((None, tq, w), lambda bi, j: (bi, j, 0))
    kspec = lambda back: pl.BlockSpec((None, tq, w), lambda bi, j: (bi, jnp.maximum(j - back, 0), 0))
    return pl.pallas_call(
        _attn_body,
        grid=(b, s // tq),
        in_specs=[qspec, kspec(2), kspec(1), kspec(0), kspec(2), kspec(1), kspec(0),
                  pl.BlockSpec(bias.shape, lambda bi, j: (0, 0, 0))],
        out_specs=qspec,
        out_shape=jax.ShapeDtypeStruct((b, s, w), BF16),
        compiler_params=_cparams("parallel", "arbitrary"),
        name="band_attn",
    )(qa, ka, ka, ka, va, va, va, bias)


def _gelu_tanh(x):
    return 0.5 * x * (1.0 + jnp.tanh(math.sqrt(2.0 / math.pi) * (x + 0.044715 * (x * x * x))))


def _ssm_body(u_ref, bm_ref, cm_ref, ar_ref, ai_ref, d_ref, wg_ref, bg_ref, y_ref,
              st_ref, bu_ref, ut_ref, yt_ref):
    nb = st_ref.shape[1]
    ts = u_ref.shape[0]
    nc = SSM_COMPLEX
    w = SSM_WIDTH
    w_ct = w // LANES

    @pl.when(pl.program_id(0) == 0)
    def _():
        st_ref[...] = jnp.zeros_like(st_ref)

    for bi in range(nb):
        for ct in range(w_ct):
            lo = bi * w + ct * LANES
            ut_ref.at[ct][pl.ds(bi, ts, stride=nb), :] = u_ref[:, lo:lo + LANES]
    u_tb = jnp.concatenate([ut_ref[ct] for ct in range(w_ct)], axis=1)
    n_ct = 2 * nc // LANES
    bu = jnp.dot(u_tb.astype(BF16), bm_ref[...], preferred_element_type=F32)
    for ct in range(n_ct):
        bu_ref[ct] = bu[:, ct * LANES:(ct + 1) * LANES]

    tiles_per_block = 4
    half = nc // LANES
    for t0 in range(0, half, tiles_per_block):
        tiles = range(t0, t0 + tiles_per_block)
        ar = [jnp.broadcast_to(ar_ref[:, ct * LANES:(ct + 1) * LANES], (nb, LANES)) for ct in tiles]
        ai = [jnp.broadcast_to(ai_ref[:, ct * LANES:(ct + 1) * LANES], (nb, LANES)) for ct in tiles]

        def step(t, carry):
            rows = pl.ds(pl.multiple_of(t * nb, nb), nb)
            out = []
            for k, ct in enumerate(tiles):
                xr, xi = carry[2 * k], carry[2 * k + 1]
                nr = ar[k] * xr - ai[k] * xi + bu_ref[ct, rows, :]
                ni = ar[k] * xi + ai[k] * xr + bu_ref[half + ct, rows, :]
                bu_ref[ct, rows, :] = nr
                bu_ref[half + ct, rows, :] = ni
                out += [nr, ni]
            return tuple(out)

        init = []
        for ct in tiles:
            init += [st_ref[0, :, ct * LANES:(ct + 1) * LANES], st_ref[1, :, ct * LANES:(ct + 1) * LANES]]
        fin = lax.fori_loop(0, ts, step, tuple(init), unroll=4)
        for k, ct in enumerate(tiles):
            st_ref[0, :, ct * LANES:(ct + 1) * LANES] = fin[2 * k]
            st_ref[1, :, ct * LANES:(ct + 1) * LANES] = fin[2 * k + 1]

    states = jnp.concatenate([bu_ref[ct] for ct in range(n_ct)], axis=1).astype(BF16)
    u_tb = jnp.concatenate([ut_ref[ct] for ct in range(w_ct)], axis=1)
    y = _gelu_tanh(jnp.dot(states, cm_ref[...], preferred_element_type=F32) + d_ref[...] * u_tb)
    z = jnp.dot(y.astype(BF16), wg_ref[...], preferred_element_type=F32) + bg_ref[...]
    y = y * _sigmoid(z)
    for ct in range(w_ct):
        yt_ref[ct] = y[:, ct * LANES:(ct + 1) * LANES]
    for bi in range(nb):
        for ct in range(w_ct):
            lo = bi * w + ct * LANES
            y_ref[:, lo:lo + LANES] = yt_ref.at[ct][pl.ds(bi, ts, stride=nb), :].astype(BF16)


def _ssm_params(a_re, a_im, log_dt, b_re, b_im, c_re, c_im):
    g, p, cg = b_re.shape
    lam = lax.complex(a_re.astype(F32), a_im.astype(F32))
    dt = jnp.exp(log_dt.astype(F32))[:, None]
    a_bar = jnp.exp(lam * dt)
    b_bar = ((a_bar - 1.0) / lam)[:, :, None] * lax.complex(b_re.astype(F32), b_im.astype(F32))
    eye = jnp.eye(g, dtype=F32)
    b_blk = lambda m: jnp.einsum('gpc,gh->gchp', m, eye).reshape(g * cg, g * p)
    bm = jnp.concatenate([b_blk(jnp.real(b_bar)), b_blk(jnp.imag(b_bar))], axis=1)
    c_blk = lambda m: jnp.einsum('gcp,gh->gphc', m.astype(F32), eye).reshape(g * p, g * cg)
    cm = jnp.concatenate([c_blk(c_re), -c_blk(c_im)], axis=0)
    ar = jnp.real(a_bar).reshape(1, g * p)
    ai = jnp.imag(a_bar).reshape(1, g * p)
    return bm.astype(BF16), cm.astype(BF16), ar, ai


def _ssm(us, nb, bm, cm, ar, ai, d_skip, w_glu, b_glu):
    s_len, wide = us.shape
    w = SSM_WIDTH
    ts = SSM_TS
    full = lambda a: pl.BlockSpec(a.shape, lambda i: (0,) * a.ndim)
    d2 = d_skip.reshape(1, w).astype(F32)
    bg2 = b_glu.reshape(1, w).astype(F32)
    wg = w_glu.astype(BF16)
    return pl.pallas_call(
        _ssm_body,
        grid=(s_len // ts,),
        in_specs=[pl.BlockSpec((ts, wide), lambda i: (i, 0)),
                  full(bm), full(cm), full(ar), full(ai), full(d2), full(wg), full(bg2)],
        out_specs=pl.BlockSpec((ts, wide), lambda i: (i, 0)),
        out_shape=jax.ShapeDtypeStruct((s_len, wide), BF16),
        scratch_shapes=[pltpu.VMEM((2, nb, SSM_COMPLEX), F32),
                        pltpu.VMEM((2 * SSM_COMPLEX // LANES, ts * nb, LANES), F32),
                        pltpu.VMEM((w // LANES, ts * nb, LANES), F32),
                        pltpu.VMEM((w // LANES, ts * nb, LANES), F32)],
        compiler_params=_cparams("arbitrary"),
        name="s5",
    )(us, bm, cm, ar, ai, d2, wg, bg2)


def _ret_body(q_ref, k_ref, v_ref, g_ref, gn_ref, dec_ref, xi_ref, zeta_ref, cd_ref, avg_ref, o_ref, st_ref):
    @pl.when(pl.program_id(1) == 0)
    def _():
        st_ref[...] = jnp.zeros_like(st_ref)

    lane = lax.broadcasted_iota(jnp.int32, (1, LANES), 1)
    low = lane < HEAD_DIM
    avg = avg_ref[...]

    def head_mean(t):
        hi = t.astype(BF16)
        lo = (t - hi.astype(F32)).astype(BF16)
        return jnp.dot(hi, avg, preferred_element_type=F32) + jnp.dot(lo, avg, preferred_element_type=F32)

    for hp in range(RET_HEADS // 2):
        sl = slice(hp * LANES, (hp + 1) * LANES)
        q2 = q_ref[:, sl]
        k2 = k_ref[:, sl]
        v2 = v_ref[:, sl]
        parts = []
        for sub in range(2):
            keep = low if sub == 0 else jnp.logical_not(low)
            k_one = jnp.where(keep, k2, jnp.zeros_like(k2))
            a = lax.dot_general(q2, k_one, (((1,), (1,)), ((), ())), preferred_element_type=F32)
            a = a * dec_ref[2 * hp + sub]
            parts.append(jnp.dot(a.astype(BF16), v2, preferred_element_type=F32))
        inner = jnp.where(low, parts[0], parts[1])
        st = st_ref[hp]
        qx = (q2.astype(F32) * xi_ref[hp]).astype(BF16)
        y = inner + jnp.dot(qx, st.astype(BF16), preferred_element_type=F32)
        kz = (k2.astype(F32) * zeta_ref[hp]).astype(BF16)
        kv = lax.dot_general(kz, v2, (((0,), (0,)), ((), ())), preferred_element_type=F32)
        blocks = cd_ref[hp]
        st_ref[hp] = blocks * st + jnp.where(blocks > 0.0, kv, 0.0)
        yc = y - head_mean(y)
        var = head_mean(yc * yc)
        yn = yc * lax.rsqrt(var + NORM_EPS) * gn_ref[:, sl]
        g = g_ref[:, sl]
        o_ref[:, sl] = (g * _sigmoid(g) * yn).astype(BF16)


def _ret_tables(t_len):
    n_pairs = RET_HEADS // 2
    log_gamma = jnp.log1p(-jnp.exp2(-5.0 - jnp.arange(RET_HEADS, dtype=F32)))
    t = jnp.arange(t_len, dtype=F32)
    diff = t[:, None] - t[None, :]
    dec = jnp.where(diff >= 0, jnp.exp(log_gamma[:, None, None] * jnp.maximum(diff, 0.0)), 0.0)
    per_lane = lambda m: jnp.repeat(m.reshape(n_pairs, 2, -1), HEAD_DIM, axis=1).transpose(0, 2, 1)
    xi = per_lane(jnp.exp(log_gamma[:, None] * (t + 1.0)))
    zeta = per_lane(jnp.exp(log_gamma[:, None] * (t_len - 1 - t)))
    lane_head = jnp.arange(LANES) // HEAD_DIM
    same = (lane_head[:, None] == lane_head[None, :]).astype(F32)
    cd_lane = jnp.repeat(jnp.exp(log_gamma * t_len).reshape(n_pairs, 2), HEAD_DIM, axis=1)
    cd = cd_lane[:, :, None] * same[None]
    avg = (same / HEAD_DIM).astype(BF16)
    return dec, xi, zeta, cd, avg


def _retention(qr, kr, vr, gr, gn_g, tables):
    b, s, w = qr.shape
    t_len = RET_T
    dec, xi, zeta, cd, avg = tables
    tok = pl.BlockSpec((None, t_len, w), lambda bi, j: (bi, j, 0))
    full = lambda a: pl.BlockSpec(a.shape, lambda bi, j: (0,) * a.ndim)
    gn2 = gn_g.reshape(1, w).astype(F32)
    return pl.pallas_call(
        _ret_body,
        grid=(b, s // t_len),
        in_specs=[tok, tok, tok, tok, full(gn2), full(dec), full(xi), full(zeta), full(cd), full(avg)],
        out_specs=tok,
        out_shape=jax.ShapeDtypeStruct((b, s, w), BF16),
        scratch_shapes=[pltpu.VMEM((RET_HEADS // 2, LANES, LANES), F32)],
        compiler_params=_cparams("parallel", "arbitrary"),
        name="retention",
    )(qr, kr, vr, gr, gn2, dec, xi, zeta, cd, avg)


def _store_row_tiled(ref, val):
    rows = val.shape[0]
    for t in range(ROW_TILES):
        ref[pl.ds(t, rows, stride=ROW_TILES), :] = val[:, t * LANES:(t + 1) * LANES]


def _load_row_tiled(ref, first, rows):
    return jnp.concatenate(
        [ref[pl.ds(first * ROW_TILES + t, rows, stride=ROW_TILES), :] for t in range(ROW_TILES)], axis=1)


def _outproj_body(x_ref, ya_ref, ys_ref, yr_ref, w_ref, g1_ref, sc_ref, sh_ref, ng_ref, wr_ref, br_ref,
                  xo_ref, h2_ref, route_ref):
    a = ATTN_WIDTH
    mixed = jnp.dot(ya_ref[...], w_ref[0:a, :], preferred_element_type=F32)
    mixed += jnp.dot(ys_ref[...], w_ref[a:a + SSM_WIDTH, :], preferred_element_type=F32)
    mixed += jnp.dot(yr_ref[...], w_ref[a + SSM_WIDTH:, :], preferred_element_type=F32)
    x = x_ref[...] + g1_ref[...] * mixed
    xo_ref[...] = x
    h2 = _modulated_norm(x, ng_ref[...], sc_ref[...], sh_ref[...])
    _store_row_tiled(h2_ref, h2)

    h_hi = h2.astype(BF16)
    h_lo = (h2 - h_hi.astype(F32)).astype(BF16)
    r_hi = jnp.dot(h_hi, wr_ref[...], preferred_element_type=F32)
    r_lo = jnp.dot(h_lo, wr_ref[...], preferred_element_type=F32)
    logits = (r_hi[:, :LANES] + ((r_hi[:, LANES:] + r_lo[:, :LANES]) + r_lo[:, LANES:])
              + br_ref[...])
    lane = lax.broadcasted_iota(jnp.int32, logits.shape, 1).astype(F32)
    big = float(LANES)

    def first_argmax(vals):
        m = jnp.max(vals, axis=-1, keepdims=True)
        return m, jnp.min(jnp.where(vals == m, lane, big), axis=-1, keepdims=True)

    gl = jnp.where(lane < N_GROUPS, logits, -jnp.inf)
    gmax, gidx = first_argmax(gl)
    group_w = 1.0 / jnp.sum(jnp.exp(gl - gmax), axis=-1, keepdims=True)
    e_lo = N_GROUPS + gidx * EXPERTS_PER_GROUP
    el = jnp.where((lane >= e_lo) & (lane < e_lo + EXPERTS_PER_GROUP), logits, -jnp.inf)
    v1, i1 = first_argmax(el)
    v2, i2 = first_argmax(jnp.where(lane == i1, -jnp.inf, el))
    t2 = jnp.exp(v2 - v1)
    w1 = group_w / (1.0 + t2)
    w2 = group_w * t2 / (1.0 + t2)
    route = jnp.where(lane == 0, i1 - N_GROUPS,
                      jnp.where(lane == 1, i2 - N_GROUPS,
                                jnp.where(lane == 2, w1, jnp.where(lane == 3, w2, 0.0))))
    route_ref[...] = route


def _outproj_router(x, ya, ys_tm, yr, w_out_bf16, g1, sc2, sh2, ng, w_route, b_route, ts):
    b, s, d = x.shape
    tok = lambda width: pl.BlockSpec((None, ts, width), lambda bi, j: (bi, j, 0))
    per_b = pl.BlockSpec((None, 1, d), lambda bi, j: (bi, 0, 0))
    full = lambda a: pl.BlockSpec(a.shape, lambda bi, j: (0,) * a.ndim)
    nj = s // ts
    return pl.pallas_call(
        _outproj_body,
        grid=(b, nj),
        in_specs=[tok(d), tok(ATTN_WIDTH),
                  pl.BlockSpec((ts, SSM_WIDTH), lambda bi, j: (j, bi)),
                  tok(RET_WIDTH), full(w_out_bf16), per_b, per_b, per_b, full(ng),
                  full(w_route), full(b_route)],
        out_specs=[tok(d),
                   pl.BlockSpec((ts * ROW_TILES, LANES), lambda bi, j: (bi * nj + j, 0)),
                   pl.BlockSpec((None, ts, LANES), lambda bi, j: (bi, j, 0))],
        out_shape=[jax.ShapeDtypeStruct((b, s, d), F32),
                   jax.ShapeDtypeStruct((b * s * ROW_TILES, LANES), F32),
                   jax.ShapeDtypeStruct((b, s, LANES), F32)],
        compiler_params=_cparams("parallel", "arbitrary"),
        name="outproj_router",
    )(x, ya, ys_tm, yr, w_out_bf16, g1, sc2, sh2, ng, w_route, b_route)


def _slot_plan(first_id, second_id, n_slots):
    tm = MOE_TM
    experts = jnp.arange(N_EXPERTS, dtype=jnp.int32)[:, None]
    first = (first_id[None, :] == experts).astype(jnp.int32)
    second = (second_id[None, :] == experts).astype(jnp.int32)
    both = first + second
    csum = jnp.cumsum(both, axis=1)
    counts = csum[:, -1]
    padded = ((counts + tm - 1) // tm) * tm
    ends = jnp.cumsum(padded)
    starts = ends - padded
    base = starts[:, None] + (csum - both)
    dest_first = jnp.sum(first * base, axis=0).astype(jnp.int32)
    dest_second = jnp.sum(second * (base + first), axis=0).astype(jnp.int32)
    n_tiles = n_slots // tm
    tile_row = jnp.arange(n_tiles, dtype=jnp.int32) * tm
    tile_expert = jnp.minimum(jnp.sum((tile_row[:, None] >= ends[None, :]).astype(jnp.int32), axis=1),
                              N_EXPERTS - 1).astype(jnp.int32)
    used_tiles = (ends[-1] // tm).astype(jnp.int32)
    group_last = jnp.where(padded > 0, ends - tm, n_slots)
    tail = jnp.where(tile_row >= ends[-1], tile_row, n_slots)
    fill = jnp.sort(jnp.concatenate([group_last, tail]).astype(jnp.int32))
    n_fill = jnp.sum(fill < n_slots).astype(jnp.int32)
    meta = jnp.stack([n_fill, used_tiles]).astype(jnp.int32)
    return (dest_first, dest_second), tile_expert, fill, meta


INPROJ_STAGES = 8
DMA_UNROLL = 8


def _dispatch_body(dest0_ref, dest1_ref, fill_ref, meta_ref, h2_ref, xs_hbm, zero_ref, sem):
    step = pl.program_id(0)
    tile_rows = MOE_TM * ROW_TILES
    tokens = h2_ref.shape[0] // ROW_TILES

    def fill_copy(i):
        row = pl.multiple_of(fill_ref[i] * ROW_TILES, tile_rows)
        return pltpu.make_async_copy(zero_ref, xs_hbm.at[pl.ds(row, tile_rows), :], sem)

    @pl.when(step == 0)
    def _():
        zero_ref[...] = jnp.zeros_like(zero_ref)
        n_fill = meta_ref[0]
        lax.fori_loop(0, n_fill, lambda i, c: (fill_copy(i).start(), c)[1], 0)
        lax.fori_loop(0, n_fill, lambda i, c: (fill_copy(i).wait(), c)[1], 0)

    base = step * tokens
    dest_refs = (dest0_ref, dest1_ref)

    def row_copy(i, choice):
        src = pl.multiple_of(i * ROW_TILES, ROW_TILES)
        dst = pl.multiple_of(dest_refs[choice][base + i] * ROW_TILES, ROW_TILES)
        return pltpu.make_async_copy(h2_ref.at[pl.ds(src, ROW_TILES), :],
                                     xs_hbm.at[pl.ds(dst, ROW_TILES), :], sem)

    def start(i, c):
        row_copy(i, 0).start(priority=0)
        row_copy(i, 1).start(priority=1)
        return c

    def wait(i, c):
        row_copy(i, 0).wait()
        row_copy(i, 1).wait()
        return c

    lax.fori_loop(0, tokens, start, 0, unroll=DMA_UNROLL)
    lax.fori_loop(0, tokens, wait, 0, unroll=DMA_UNROLL)


def _dispatch(h2_tiled, dest, fill, meta, n_slots):
    tokens = 512
    rows = tokens * ROW_TILES
    return pl.pallas_call(
        _dispatch_body,
        grid_spec=pltpu.PrefetchScalarGridSpec(
            num_scalar_prefetch=4,
            grid=(h2_tiled.shape[0] // rows,),
            in_specs=[pl.BlockSpec((rows, LANES), lambda i, d0, d1, f, m: (i, 0))],
            out_specs=pl.BlockSpec(memory_space=pl.ANY),
            scratch_shapes=[pltpu.VMEM((MOE_TM * ROW_TILES, LANES), F32), pltpu.SemaphoreType.DMA],
        ),
        out_shape=jax.ShapeDtypeStruct((n_slots * ROW_TILES, LANES), F32),
        compiler_params=pltpu.CompilerParams(dimension_semantics=("arbitrary",), has_side_effects=True,
                                             vmem_limit_bytes=VMEM_LIMIT),
        name="moe_dispatch",
    )(*dest, fill, meta, h2_tiled)


def _experts_body(te_ref, meta_ref, xs_ref, wg_ref, wu_ref, wd_ref, ys_ref):
    tm = MOE_TM

    @pl.when(pl.program_id(0) < meta_ref[1])
    def _():
        x = _load_row_tiled(xs_ref, 0, tm).astype(BF16)
        hg = jnp.dot(x, wg_ref[...].astype(BF16), preferred_element_type=F32)
        hu = jnp.dot(x, wu_ref[...].astype(BF16), preferred_element_type=F32)
        act = (hg * _sigmoid(hg) * hu).astype(BF16)
        _store_row_tiled(ys_ref, jnp.dot(act, wd_ref[...].astype(BF16), preferred_element_type=F32))

    @pl.when(pl.program_id(0) >= meta_ref[1])
    def _():
        ys_ref[...] = jnp.zeros_like(ys_ref)


def _experts(xs, tile_expert, meta, w_gate, w_up, w_down, layer):
    n_tiles = tile_expert.shape[0]
    rows = MOE_TM * ROW_TILES
    d, f = w_gate.shape[2], w_gate.shape[3]
    return pl.pallas_call(
        _experts_body,
        grid_spec=pltpu.PrefetchScalarGridSpec(
            num_scalar_prefetch=2,
            grid=(n_tiles,),
            in_specs=[pl.BlockSpec((rows, LANES), lambda i, te, mt: (i, 0)),
                      pl.BlockSpec((None, None, d, f), lambda i, te, mt: (layer, te[i], 0, 0)),
                      pl.BlockSpec((None, None, d, f), lambda i, te, mt: (layer, te[i], 0, 0)),
                      pl.BlockSpec((None, None, f, d), lambda i, te, mt: (layer, te[i], 0, 0))],
            out_specs=pl.BlockSpec((rows, LANES), lambda i, te, mt: (i, 0)),
        ),
        out_shape=jax.ShapeDtypeStruct(xs.shape, F32),
        compiler_params=_cparams("arbitrary"),
        name="moe_experts",
    )(tile_expert, meta, xs, w_gate, w_up, w_down)


def _combine_body(dest0_ref, dest1_ref, x_ref, route_ref, g2_ref, fg_ref, ys_hbm, o_ref, buf_ref, sem, *,
                  final_norm):
    tm = x_ref.shape[0]
    base = (pl.program_id(0) * pl.num_programs(1) + pl.program_id(1)) * tm

    dest_refs = (dest0_ref, dest1_ref)

    def row_copy(i, choice):
        src = pl.multiple_of(dest_refs[choice][base + i] * ROW_TILES, ROW_TILES)
        dst = pl.multiple_of((choice * tm + i) * ROW_TILES, ROW_TILES)
        return pltpu.make_async_copy(ys_hbm.at[pl.ds(src, ROW_TILES), :],
                                     buf_ref.at[pl.ds(dst, ROW_TILES), :], sem)

    def start(i, c):
        row_copy(i, 0).start(priority=0)
        row_copy(i, 1).start(priority=1)
        return c

    def wait(i, c):
        row_copy(i, 0).wait()
        row_copy(i, 1).wait()
        return c

    lax.fori_loop(0, tm, start, 0, unroll=DMA_UNROLL)
    lax.fori_loop(0, tm, wait, 0, unroll=DMA_UNROLL)
    route = route_ref[...]
    y = route[:, 2:3] * _load_row_tiled(buf_ref, 0, tm) + route[:, 3:4] * _load_row_tiled(buf_ref, tm, tm)
    x = x_ref[...] + g2_ref[...] * y
    if final_norm:
        ms = jnp.mean(x * x, axis=-1, keepdims=True)
        x = x * lax.rsqrt(ms + NORM_EPS) * fg_ref[...]
    o_ref[...] = x


def _combine(x, route, g2, final_g, ys, dest, tm, final_norm):
    b, s, d = x.shape
    tok = lambda width: pl.BlockSpec((None, tm, width), lambda bi, j, d0, d1: (bi, j, 0))
    return pl.pallas_call(
        functools.partial(_combine_body, final_norm=final_norm),
        grid_spec=pltpu.PrefetchScalarGridSpec(
            num_scalar_prefetch=2,
            grid=(b, s // tm),
            in_specs=[tok(d), tok(LANES),
                      pl.BlockSpec((None, 1, d), lambda bi, j, d0, d1: (bi, 0, 0)),
                      pl.BlockSpec((1, d), lambda bi, j, d0, d1: (0, 0)),
                      pl.BlockSpec(memory_space=pl.ANY)],
            out_specs=tok(d),
            scratch_shapes=[pltpu.VMEM((2 * tm * ROW_TILES, LANES), F32), pltpu.SemaphoreType.DMA],
        ),
        out_shape=jax.ShapeDtypeStruct((b, s, d), F32),
        compiler_params=_cparams("arbitrary", "arbitrary"),
        name="moe_combine",
    )(*dest, x, route, g2, final_g, ys)


def _combine_inproj_body(dest0_ref, dest1_ref, x_ref, route_ref, g2_ref, ys_hbm, sc_ref, sh_ref, g_ref, w_ref, cos_ref,
                         sin_ref, xo_ref, qa_ref, ka_ref, va_ref, us_ref, qr_ref, kr_ref, vr_ref, gr_ref,
                         buf_ref, sems):
    tm = x_ref.shape[0]
    n_tiles = pl.num_programs(0) * pl.num_programs(1)
    tile = pl.program_id(0) * pl.num_programs(1) + pl.program_id(1)
    slot = tile % 2

    dest_refs = (dest0_ref, dest1_ref)

    def row_copy(t, s, i, choice):
        src = pl.multiple_of(dest_refs[choice][t * tm + i] * ROW_TILES, ROW_TILES)
        dst = pl.multiple_of(((2 * s + choice) * tm + i) * ROW_TILES, ROW_TILES)
        return pltpu.make_async_copy(ys_hbm.at[pl.ds(src, ROW_TILES), :],
                                     buf_ref.at[pl.ds(dst, ROW_TILES), :], sems.at[s])

    def start_rows(t, s, lo, hi):
        def start(i, c):
            row_copy(t, s, i, 0).start(priority=0)
            row_copy(t, s, i, 1).start(priority=1)
            return c
        lax.fori_loop(lo, hi, start, 0, unroll=DMA_UNROLL)

    @pl.when(tile == 0)
    def _():
        start_rows(tile, slot, 0, tm)

    chunk = tm // INPROJ_STAGES
    nxt = jnp.minimum(tile + 1, n_tiles - 1)

    def prefetch_chunk(k):
        for i in range(k * chunk, (k + 1) * chunk):
            row_copy(nxt, 1 - slot, i, 0).start(priority=0)
            row_copy(nxt, 1 - slot, i, 1).start(priority=1)

    def wait_tile(t, s):
        def wait(i, c):
            row_copy(t, s, i, 0).wait()
            row_copy(t, s, i, 1).wait()
            return c
        lax.fori_loop(0, tm, wait, 0, unroll=DMA_UNROLL)

    wait_tile(tile, slot)
    route = route_ref[...]
    first = 2 * slot * tm
    y = (route[:, 2:3] * _load_row_tiled(buf_ref, first, tm)
         + route[:, 3:4] * _load_row_tiled(buf_ref, first + tm, tm))
    x = x_ref[...] + g2_ref[...] * y
    xo_ref[...] = x
    _inproj_project(x, sc_ref, sh_ref, g_ref, w_ref, cos_ref, sin_ref,
                    qa_ref, ka_ref, va_ref, us_ref, qr_ref, kr_ref, vr_ref, gr_ref, between=prefetch_chunk)

    @pl.when(tile == n_tiles - 1)
    def _():
        wait_tile(nxt, 1 - slot)


def _combine_inproj(x, route, g2, ys, dest, sc, sh, g, w_bf16, cos_t, sin_t, ts):
    b, s, d = x.shape
    tok = lambda width: pl.BlockSpec((None, ts, width), lambda bi, j, d0, d1: (bi, j, 0))
    per_b = pl.BlockSpec((None, 1, d), lambda bi, j, d0, d1: (bi, 0, 0))
    rope = pl.BlockSpec((ts, RET_WIDTH), lambda bi, j, d0, d1: (j, 0))
    sds = lambda width, dt: jax.ShapeDtypeStruct((b, s, width), dt)
    return pl.pallas_call(
        _combine_inproj_body,
        grid_spec=pltpu.PrefetchScalarGridSpec(
            num_scalar_prefetch=2,
            grid=(b, s // ts),
            in_specs=[tok(d), tok(LANES), per_b, pl.BlockSpec(memory_space=pl.ANY), per_b, per_b,
                      pl.BlockSpec((1, d), lambda bi, j, d0, d1: (0, 0)),
                      pl.BlockSpec(w_bf16.shape, lambda bi, j, d0, d1: (0, 0)),
                      rope, rope],
            out_specs=[tok(d), tok(ATTN_WIDTH), tok(ATTN_WIDTH), tok(ATTN_WIDTH),
                       pl.BlockSpec((ts, SSM_WIDTH), lambda bi, j, d0, d1: (j, bi)),
                       tok(RET_WIDTH), tok(RET_WIDTH), tok(RET_WIDTH), tok(RET_WIDTH)],
            scratch_shapes=[pltpu.VMEM((2 * 2 * ts * ROW_TILES, LANES), F32), pltpu.SemaphoreType.DMA((2,))],
        ),
        out_shape=[sds(d, F32), sds(ATTN_WIDTH, BF16), sds(ATTN_WIDTH, BF16), sds(ATTN_WIDTH, BF16),
                   jax.ShapeDtypeStruct((s, b * SSM_WIDTH), F32),
                   sds(RET_WIDTH, BF16), sds(RET_WIDTH, BF16), sds(RET_WIDTH, BF16), sds(RET_WIDTH, F32)],
        compiler_params=_cparams("arbitrary", "arbitrary"),
        name="combine_inproj",
    )(*dest, x, route, g2, ys, sc, sh, g, w_bf16, cos_t, sin_t)


def _rope_tables(s):
    half = HEAD_DIM // 2
    inv_freq = ROPE_BASE ** (-jnp.arange(half, dtype=F32) / half)
    ang = jnp.arange(s, dtype=F32)[:, None] * inv_freq[None, :]
    cos = jnp.tile(jnp.cos(ang), (1, 2 * RET_HEADS))
    sin = jnp.sin(ang)
    sin = jnp.tile(jnp.concatenate([-sin, sin], axis=1), (1, RET_HEADS))
    return cos, sin


def _trunk(x, c, norm1_g, norm2_g, w_ada, b_ada, w_in, attn_rel_bias, ssm_a_re, ssm_a_im, ssm_log_dt,
           ssm_b_re, ssm_b_im, ssm_c_re, ssm_c_im, ssm_d, ssm_w_glu, ssm_b_glu, ret_gn_g, w_out,
           moe_w_group, moe_b_group, moe_w_expert, moe_b_expert, moe_w_gate, moe_w_up, moe_w_down, final_g,
           *, row_tile):
    b, s, d = x.shape
    n_layers = w_in.shape[0]
    n_slots = 2 * b * s + N_EXPERTS * MOE_TM
    mod = _adaln(c, w_ada, b_ada).reshape(n_layers, b, 6, 1, d)
    cos_t, sin_t = _rope_tables(s)
    ret_tables = _ret_tables(RET_T)
    w_in_b = w_in.astype(BF16)
    w_out_b = w_out.astype(BF16)
    n_route = N_GROUPS + N_EXPERTS
    w_route = jnp.pad(jnp.concatenate([moe_w_group, moe_w_expert], axis=-1).astype(F32),
                      ((0, 0), (0, 0), (0, LANES - n_route)))
    w_route_hi = w_route.astype(BF16)
    w_route = jnp.concatenate([w_route_hi, (w_route - w_route_hi.astype(F32)).astype(BF16)], axis=-1)
    b_route = jnp.pad(jnp.concatenate([moe_b_group, moe_b_expert], axis=-1).astype(F32),
                      ((0, 0), (0, LANES - n_route))).reshape(n_layers, 1, LANES)
    fg = final_g.reshape(1, d).astype(F32)
    projected = _inproj(x, mod[0, :, 1], mod[0, :, 0], norm1_g[0].reshape(1, d), w_in_b[0], cos_t, sin_t, row_tile)
    for i in range(n_layers):
        _, _, g1, sh2, sc2, g2 = (mod[i, :, k] for k in range(6))
        qa, ka, va, us, qr, kr, vr, gr = projected
        y_a = _attention(qa, ka, va, _attn_bias_table(attn_rel_bias[i], ATTN_TQ))
        bm, cm, ar, ai = _ssm_params(ssm_a_re[i], ssm_a_im[i], ssm_log_dt[i], ssm_b_re[i], ssm_b_im[i],
                                     ssm_c_re[i], ssm_c_im[i])
        y_s = _ssm(us, b, bm, cm, ar, ai, ssm_d[i], ssm_w_glu[i], ssm_b_glu[i])
        y_r = _retention(qr, kr, vr, gr, ret_gn_g[i], ret_tables)
        x, h2, route = _outproj_router(x, y_a, y_s, y_r, w_out_b[i], g1, sc2, sh2,
                                       norm2_g[i].reshape(1, d), w_route[i], b_route[i], row_tile)
        first_id = route[:, :, 0].astype(jnp.int32).reshape(b * s)
        second_id = route[:, :, 1].astype(jnp.int32).reshape(b * s)
        dest, tile_expert, fill, meta = _slot_plan(first_id, second_id, n_slots)
        xs = _dispatch(h2, dest, fill, meta, n_slots)
        ys = _experts(xs, tile_expert, meta, moe_w_gate, moe_w_up, moe_w_down, i)
        if i == n_layers - 1:
            x = _combine(x, route, g2, fg, ys, dest, 256, final_norm=True)
        else:
            x, *projected = _combine_inproj(x, route, g2, ys, dest, mod[i + 1, :, 1], mod[i + 1, :, 0],
                                            norm1_g[i + 1].reshape(1, d), w_in_b[i + 1], cos_t, sin_t, row_tile)
    return x


def kernel(x, c, norm1_g, norm2_g, w_ada, b_ada, w_in, attn_rel_bias, ssm_a_re, ssm_a_im, ssm_log_dt, ssm_b_re, ssm_b_im, ssm_c_re, ssm_c_im, ssm_d, ssm_w_glu, ssm_b_glu, ret_gn_g, w_out, moe_w_group, moe_b_group, moe_w_expert, moe_b_expert, moe_w_gate, moe_w_up, moe_w_down, final_g):
    return _trunk(x, c, norm1_g, norm2_g, w_ada, b_ada, w_in, attn_rel_bias, ssm_a_re, ssm_a_im, ssm_log_dt,
                  ssm_b_re, ssm_b_im, ssm_c_re, ssm_c_im, ssm_d, ssm_w_glu, ssm_b_glu, ret_gn_g, w_out,
                  moe_w_group, moe_b_group, moe_w_expert, moe_b_expert, moe_w_gate, moe_w_up, moe_w_down,
                  final_g, row_tile=512)
```

```python
import functools
import math

import jax
import jax.numpy as jnp
import numpy as np
from jax import lax
from jax.experimental import pallas as pl
from jax.experimental.pallas import tpu as pltpu

F32 = jnp.float32
BF16 = jnp.bfloat16

D_MODEL = 1024
N_LAYERS = 4
CHUNK = 64
HEAD_DIM = 64
NORM_EPS = 1e-6

ATTN_HEADS = 8
ATTN_WIDTH = ATTN_HEADS * HEAD_DIM
LEFT_CHUNKS = 8
MAX_REL = 128
REL_TABLE = MAX_REL + CHUNK

SSM_WIDTH = 256
SSM_GROUP = 16
SSM_GROUPS = 16
SSM_STATE = 64
SSM_COMPLEX = SSM_GROUPS * SSM_STATE

RET_HEADS = 4
RET_WIDTH = RET_HEADS * HEAD_DIM
ROPE_BASE = 10000.0

N_GROUPS = 4
EXPERTS_PER_GROUP = 8
N_EXPERTS = N_GROUPS * EXPERTS_PER_GROUP
EXPERT_FF = 256

SUBLANES = 8
LANES = 128
VMEM_LIMIT = 56 * 1024 * 1024

ATTN_TQ = 256
RET_T = 256
SSM_TS = 64
MOE_TM = 512
ROW_TILES = D_MODEL // LANES
NEG_BIG = -1e30
LOG2_E = math.log2(math.e)


def _sigmoid(x):
    return 1.0 / (1.0 + jnp.exp(-x))


def _cparams(*sem):
    return pltpu.CompilerParams(dimension_semantics=sem, vmem_limit_bytes=VMEM_LIMIT)


def _adaln_body(c_ref, w_ref, b_ref, o_ref):
    c = c_ref[...]
    cond = c * _sigmoid(c)
    o_ref[...] = jnp.dot(cond, w_ref[...], preferred_element_type=F32,
                         precision=lax.Precision.HIGHEST) + b_ref[...]


def _adaln(c, w_ada, b_ada):
    n_l, d, d6 = w_ada.shape
    b = c.shape[0]
    tn = 1536
    return pl.pallas_call(
        _adaln_body,
        grid=(n_l, d6 // tn),
        in_specs=[pl.BlockSpec((b, d), lambda l, j: (0, 0)),
                  pl.BlockSpec((None, d, tn), lambda l, j: (l, 0, j)),
                  pl.BlockSpec((None, 1, tn), lambda l, j: (l, 0, j))],
        out_specs=pl.BlockSpec((None, b, tn), lambda l, j: (l, 0, j)),
        out_shape=jax.ShapeDtypeStruct((n_l, b, d6), F32),
        compiler_params=_cparams("arbitrary", "arbitrary"),
        name="adaln",
    )(c, w_ada, b_ada.reshape(n_l, 1, d6))


def _modulated_norm(x, g, sc, sh):
    ms = jnp.mean(x * x, axis=-1, keepdims=True)
    return (x * lax.rsqrt(ms + NORM_EPS) * g) * (1.0 + sc) + sh


def _inproj_body(x_ref, *refs):
    _inproj_project(x_ref[...], *refs)


def _inproj_project(x, sc_ref, sh_ref, g_ref, w_ref, cos_ref, sin_ref,
                    qa_ref, ka_ref, va_ref, us_ref, qr_ref, kr_ref, vr_ref, gr_ref, between=None):
    hb = _modulated_norm(x, g_ref[...], sc_ref[...], sh_ref[...]).astype(BF16)
    stage = iter(range(INPROJ_STAGES))

    def proj(lo, width):
        if between is not None:
            between(next(stage))
        return jnp.dot(hb, w_ref[:, lo:lo + width], preferred_element_type=F32)

    scale = HEAD_DIM ** -0.5
    a = ATTN_WIDTH
    qa_ref[...] = (proj(0, a) * (scale * LOG2_E)).astype(BF16)
    ka_ref[...] = proj(a, a).astype(BF16)
    va_ref[...] = proj(2 * a, a).astype(BF16)
    o = 3 * a
    us_ref[...] = proj(o, SSM_WIDTH)
    o += SSM_WIDTH

    cos = cos_ref[...]
    sin = sin_ref[...]
    lane = lax.broadcasted_iota(jnp.int32, cos.shape, 1)
    first_half = (lane & (HEAD_DIM // 2)) == 0

    def rotary(z):
        partner = jnp.where(first_half,
                            pltpu.roll(z, RET_WIDTH - HEAD_DIM // 2, 1),
                            pltpu.roll(z, HEAD_DIM // 2, 1))
        return z * cos + partner * sin

    r = RET_WIDTH
    qr_ref[...] = rotary(proj(o, r)).astype(BF16)
    kr_ref[...] = (rotary(proj(o + r, r)) * scale).astype(BF16)
    vr_ref[...] = proj(o + 2 * r, r).astype(BF16)
    gr_ref[...] = proj(o + 3 * r, r)


def _inproj(x, sc, sh, g, w_bf16, cos_t, sin_t, ts):
    b, s, d = x.shape
    nj = s // ts
    tok = lambda width: pl.BlockSpec((None, ts, width), lambda bi, j: (bi, j, 0))
    per_b = pl.BlockSpec((None, 1, d), lambda bi, j: (bi, 0, 0))
    rope = pl.BlockSpec((ts, RET_WIDTH), lambda bi, j: (j, 0))
    sds = lambda width, dt: jax.ShapeDtypeStruct((b, s, width), dt)
    return pl.pallas_call(
        _inproj_body,
        grid=(b, nj),
        in_specs=[tok(d), per_b, per_b,
                  pl.BlockSpec((1, d), lambda bi, j: (0, 0)),
                  pl.BlockSpec(w_bf16.shape, lambda bi, j: (0, 0)),
                  rope, rope],
        out_specs=[tok(ATTN_WIDTH), tok(ATTN_WIDTH), tok(ATTN_WIDTH),
                   pl.BlockSpec((ts, SSM_WIDTH), lambda bi, j: (j, bi)),
                   tok(RET_WIDTH), tok(RET_WIDTH), tok(RET_WIDTH), tok(RET_WIDTH)],
        out_shape=[sds(ATTN_WIDTH, BF16), sds(ATTN_WIDTH, BF16), sds(ATTN_WIDTH, BF16),
                   jax.ShapeDtypeStruct((s, b * SSM_WIDTH), F32),
                   sds(RET_WIDTH, BF16), sds(RET_WIDTH, BF16), sds(RET_WIDTH, BF16), sds(RET_WIDTH, F32)],
        compiler_params=_cparams("parallel", "arbitrary"),
        name="inproj",
    )(x, sc, sh, g, w_bf16, cos_t, sin_t)


def _attn_body(q_ref, k0_ref, k1_ref, k2_ref, v0_ref, v1_ref, v2_ref, bias_ref, o_ref):
    j = pl.program_id(1)
    tq = q_ref.shape[0]
    lane = lax.broadcasted_iota(jnp.int32, (1, LANES), 1)
    low = lane < HEAD_DIM
    ones_cols = jnp.ones((3 * tq, LANES), BF16)

    def heads(mask_start):
        if mask_start:
            col = lax.broadcasted_iota(jnp.int32, (tq, 3 * tq), 1)
            in_seq = col >= (2 - j) * tq
        for hp in range(ATTN_HEADS // 2):
            sl = slice(hp * LANES, (hp + 1) * LANES)
            q2 = q_ref[:, sl]
            k2 = jnp.concatenate([k0_ref[:, sl], k1_ref[:, sl], k2_ref[:, sl]], axis=0)
            v2 = jnp.concatenate([v0_ref[:, sl], v1_ref[:, sl], v2_ref[:, sl]], axis=0)
            v_aug = jnp.concatenate([v2, ones_cols], axis=1)
            outs = []
            for sub in range(2):
                keep = low if sub == 0 else jnp.logical_not(low)
                kz = jnp.where(keep, k2, jnp.zeros_like(k2))
                s = lax.dot_general(q2, kz, (((1,), (1,)), ((), ())), preferred_element_type=F32)
                s = s + bias_ref[2 * hp + sub]
                if mask_start:
                    s = jnp.where(in_seq, s, NEG_BIG)
                p = jnp.exp2(s - jnp.max(s, axis=-1, keepdims=True)).astype(BF16)
                r = jnp.dot(p, v_aug, preferred_element_type=F32)
                outs.append(r[:, :LANES] * (1.0 / r[:, LANES:LANES + 1]))
            o_ref[:, sl] = jnp.where(low, outs[0], outs[1]).astype(BF16)

    @pl.when(j < 2)
    def _():
        heads(True)

    @pl.when(j >= 2)
    def _():
        heads(False)


def _attn_bias_table(rel_bias, tq):
    r = np.arange(tq)[:, None]
    c = np.arange(3 * tq)[None, :]
    qc = r // CHUNK + 2 * tq // CHUNK
    kc = c // CHUNK
    band = (kc >= qc - LEFT_CHUNKS) & (kc <= qc)
    period = 4 * tq
    k = np.arange(period)
    diff = np.where(k < 3 * tq, k, k - period)
    idx = np.clip(diff - 2 * tq, -MAX_REL, CHUNK - 1) + MAX_REL
    vec = rel_bias.astype(F32)[:, idx]
    h = vec.shape[0]
    table = jnp.tile(vec, (1, tq))[:, :tq * (period - 1)].reshape(h, tq, period - 1)[:, :, :3 * tq]
    return jnp.where(jnp.asarray(band)[None], table * LOG2_E, NEG_BIG)


def _attention(qa, ka, va, bias):
    b, s, w = qa.shape
    tq = ATTN_TQ
    qspec = pl.BlockSpec((None, tq, w), lambda bi, j: (bi, j, 0))
    kspec = lambda back: pl.BlockSpec((None, tq, w), lambda bi, j: (bi, jnp.maximum(j - back, 0), 0))
    return pl.pallas_call(
        _attn_body,
        grid=(b, s // tq),
        in_specs=[qspec, kspec(2), kspec(1), kspec(0), kspec(2), kspec(1), kspec(0),
                  pl.BlockSpec(bias.shape, lambda bi, j: (0, 0, 0))],
        out_specs=qspec,
        out_shape=jax.ShapeDtypeStruct((b, s, w), BF16),
        compiler_params=_cparams("parallel", "arbitrary"),
        name="band_attn",
    )(qa, ka, ka, ka, va, va, va, bias)


def _gelu_tanh(x):
    return 0.5 * x * (1.0 + jnp.tanh(math.sqrt(2.0 / math.pi) * (x + 0.044715 * (x * x * x))))


def _ssm_body(u_ref, bm_ref, cm_ref, ar_ref, ai_ref, d_ref, wg_ref, bg_ref, y_ref,
              st_ref, bu_ref, ut_ref, yt_ref):
    nb = st_ref.shape[1]
    ts = u_ref.shape[0]
    nc = SSM_COMPLEX
    w = SSM_WIDTH
    w_ct = w // LANES

    @pl.when(pl.program_id(0) == 0)
    def _():
        st_ref[...] = jnp.zeros_like(st_ref)

    for bi in range(nb):
        for ct in range(w_ct):
            lo = bi * w + ct * LANES
            ut_ref.at[ct][pl.ds(bi, ts, stride=nb), :] = u_ref[:, lo:lo + LANES]
    u_tb = jnp.concatenate([ut_ref[ct] for ct in range(w_ct)], axis=1)
    n_ct = 2 * nc // LANES
    bu = jnp.dot(u_tb.astype(BF16), bm_ref[...], preferred_element_type=F32)
    for ct in range(n_ct):
        bu_ref[ct] = bu[:, ct * LANES:(ct + 1) * LANES]

    tiles_per_block = 4
    half = nc // LANES
    for t0 in range(0, half, tiles_per_block):
        tiles = range(t0, t0 + tiles_per_block)
        ar = [jnp.broadcast_to(ar_ref[:, ct * LANES:(ct + 1) * LANES], (nb, LANES)) for ct in tiles]
        ai = [jnp.broadcast_to(ai_ref[:, ct * LANES:(ct + 1) * LANES], (nb, LANES)) for ct in tiles]

        def step(t, carry):
            rows = pl.ds(pl.multiple_of(t * nb, nb), nb)
            out = []
            for k, ct in enumerate(tiles):
                xr, xi = carry[2 * k], carry[2 * k + 1]
                nr = ar[k] * xr - ai[k] * xi + bu_ref[ct, rows, :]
                ni = ar[k] * xi + ai[k] * xr + bu_ref[half + ct, rows, :]
                bu_ref[ct, rows, :] = nr
                bu_ref[half + ct, rows, :] = ni
                out += [nr, ni]
            return tuple(out)

        init = []
        for ct in tiles:
            init += [st_ref[0, :, ct * LANES:(ct + 1) * LANES], st_ref[1, :, ct * LANES:(ct + 1) * LANES]]
        fin = lax.fori_loop(0, ts, step, tuple(init), unroll=4)
        for k, ct in enumerate(tiles):
            st_ref[0, :, ct * LANES:(ct + 1) * LANES] = fin[2 * k]
            st_ref[1, :, ct * LANES:(ct + 1) * LANES] = fin[2 * k + 1]

    states = jnp.concatenate([bu_ref[ct] for ct in range(n_ct)], axis=1).astype(BF16)
    u_tb = jnp.concatenate([ut_ref[ct] for ct in range(w_ct)], axis=1)
    y = _gelu_tanh(jnp.dot(states, cm_ref[...], preferred_element_type=F32) + d_ref[...] * u_tb)
    z = jnp.dot(y.astype(BF16), wg_ref[...], preferred_element_type=F32) + bg_ref[...]
    y = y * _sigmoid(z)
    for ct in range(w_ct):
        yt_ref[ct] = y[:, ct * LANES:(ct + 1) * LANES]
    for bi in range(nb):
        for ct in range(w_ct):
            lo = bi * w + ct * LANES
            y_ref[:, lo:lo + LANES] = yt_ref.at[ct][pl.ds(bi, ts, stride=nb), :].astype(BF16)


def _ssm_params(a_re, a_im, log_dt, b_re, b_im, c_re, c_im):
    g, p, cg = b_re.shape
    lam = lax.complex(a_re.astype(F32), a_im.astype(F32))
    dt = jnp.exp(log_dt.astype(F32))[:, None]
    a_bar = jnp.exp(lam * dt)
    b_bar = ((a_bar - 1.0) / lam)[:, :, None] * lax.complex(b_re.astype(F32), b_im.astype(F32))
    eye = jnp.eye(g, dtype=F32)
    b_blk = lambda m: jnp.einsum('gpc,gh->gchp', m, eye).reshape(g * cg, g * p)
    bm = jnp.concatenate([b_blk(jnp.real(b_bar)), b_blk(jnp.imag(b_bar))], axis=1)
    c_blk = lambda m: jnp.einsum('gcp,gh->gphc', m.astype(F32), eye).reshape(g * p, g * cg)
    cm = jnp.concatenate([c_blk(c_re), -c_blk(c_im)], axis=0)
    ar = jnp.real(a_bar).reshape(1, g * p)
    ai = jnp.imag(a_bar).reshape(1, g * p)
    return bm.astype(BF16), cm.astype(BF16), ar, ai


def _ssm(us, nb, bm, cm, ar, ai, d_skip, w_glu, b_glu):
    s_len, wide = us.shape
    w = SSM_WIDTH
    ts = SSM_TS
    full = lambda a: pl.BlockSpec(a.shape, lambda i: (0,) * a.ndim)
    d2 = d_skip.reshape(1, w).astype(F32)
    bg2 = b_glu.reshape(1, w).astype(F32)
    wg = w_glu.astype(BF16)
    return pl.pallas_call(
        _ssm_body,
        grid=(s_len // ts,),
        in_specs=[pl.BlockSpec((ts, wide), lambda i: (i, 0)),
                  full(bm), full(cm), full(ar), full(ai), full(d2), full(wg), full(bg2)],
        out_specs=pl.BlockSpec((ts, wide), lambda i: (i, 0)),
        out_shape=jax.ShapeDtypeStruct((s_len, wide), BF16),
        scratch_shapes=[pltpu.VMEM((2, nb, SSM_COMPLEX), F32),
                        pltpu.VMEM((2 * SSM_COMPLEX // LANES, ts * nb, LANES), F32),
                        pltpu.VMEM((w // LANES, ts * nb, LANES), F32),
                        pltpu.VMEM((w // LANES, ts * nb, LANES), F32)],
        compiler_params=_cparams("arbitrary"),
        name="s5",
    )(us, bm, cm, ar, ai, d2, wg, bg2)


def _ret_body(q_ref, k_ref, v_ref, g_ref, gn_ref, dec_ref, xi_ref, zeta_ref, cd_ref, avg_ref, o_ref, st_ref):
    @pl.when(pl.program_id(1) == 0)
    def _():
        st_ref[...] = jnp.zeros_like(st_ref)

    lane = lax.broadcasted_iota(jnp.int32, (1, LANES), 1)
    low = lane < HEAD_DIM
    avg = avg_ref[...]

    def head_mean(t):
        hi = t.astype(BF16)
        lo = (t - hi.astype(F32)).astype(BF16)
        return jnp.dot(hi, avg, preferred_element_type=F32) + jnp.dot(lo, avg, preferred_element_type=F32)

    for hp in range(RET_HEADS // 2):
        sl = slice(hp * LANES, (hp + 1) * LANES)
        q2 = q_ref[:, sl]
        k2 = k_ref[:, sl]
        v2 = v_ref[:, sl]
        parts = []
        for sub in range(2):
            keep = low if sub == 0 else jnp.logical_not(low)
            k_one = jnp.where(keep, k2, jnp.zeros_like(k2))
            a = lax.dot_general(q2, k_one, (((1,), (1,)), ((), ())), preferred_element_type=F32)
            a = a * dec_ref[2 * hp + sub]
            parts.append(jnp.dot(a.astype(BF16), v2, preferred_element_type=F32))
        inner = jnp.where(low, parts[0], parts[1])
        st = st_ref[hp]
        qx = (q2.astype(F32) * xi_ref[hp]).astype(BF16)
        y = inner + jnp.dot(qx, st.astype(BF16), preferred_element_type=F32)
        kz = (k2.astype(F32) * zeta_ref[hp]).astype(BF16)
        kv = lax.dot_general(kz, v2, (((0,), (0,)), ((), ())), preferred_element_type=F32)
        blocks = cd_ref[hp]
        st_ref[hp] = blocks * st + jnp.where(blocks > 0.0, kv, 0.0)
        yc = y - head_mean(y)
        var = head_mean(yc * yc)
        yn = yc * lax.rsqrt(var + NORM_EPS) * gn_ref[:, sl]
        g = g_ref[:, sl]
        o_ref[:, sl] = (g * _sigmoid(g) * yn).astype(BF16)


def _ret_tables(t_len):
    n_pairs = RET_HEADS // 2
    log_gamma = jnp.log1p(-jnp.exp2(-5.0 - jnp.arange(RET_HEADS, dtype=F32)))
    t = jnp.arange(t_len, dtype=F32)
    diff = t[:, None] - t[None, :]
    dec = jnp.where(diff >= 0, jnp.exp(log_gamma[:, None, None] * jnp.maximum(diff, 0.0)), 0.0)
    per_lane = lambda m: jnp.repeat(m.reshape(n_pairs, 2, -1), HEAD_DIM, axis=1).transpose(0, 2, 1)
    xi = per_lane(jnp.exp(log_gamma[:, None] * (t + 1.0)))
    zeta = per_lane(jnp.exp(log_gamma[:, None] * (t_len - 1 - t)))
    lane_head = jnp.arange(LANES) // HEAD_DIM
    same = (lane_head[:, None] == lane_head[None, :]).astype(F32)
    cd_lane = jnp.repeat(jnp.exp(log_gamma * t_len).reshape(n_pairs, 2), HEAD_DIM, axis=1)
    cd = cd_lane[:, :, None] * same[None]
    avg = (same / HEAD_DIM).astype(BF16)
    return dec, xi, zeta, cd, avg


def _retention(qr, kr, vr, gr, gn_g, tables):
    b, s, w = qr.shape
    t_len = RET_T
    dec, xi, zeta, cd, avg = tables
    tok = pl.BlockSpec((None, t_len, w), lambda bi, j: (bi, j, 0))
    full = lambda a: pl.BlockSpec(a.shape, lambda bi, j: (0,) * a.ndim)
    gn2 = gn_g.reshape(1, w).astype(F32)
    return pl.pallas_call(
        _ret_body,
        grid=(b, s // t_len),
        in_specs=[tok, tok, tok, tok, full(gn2), full(dec), full(xi), full(zeta), full(cd), full(avg)],
        out_specs=tok,
        out_shape=jax.ShapeDtypeStruct((b, s, w), BF16),
        scratch_shapes=[pltpu.VMEM((RET_HEADS // 2, LANES, LANES), F32)],
        compiler_params=_cparams("parallel", "arbitrary"),
        name="retention",
    )(qr, kr, vr, gr, gn2, dec, xi, zeta, cd, avg)


def _store_row_tiled(ref, val):
    rows = val.shape[0]
    for t in range(ROW_TILES):
        ref[pl.ds(t, rows, stride=ROW_TILES), :] = val[:, t * LANES:(t + 1) * LANES]


def _load_row_tiled(ref, first, rows):
    return jnp.concatenate(
        [ref[pl.ds(first * ROW_TILES + t, rows, stride=ROW_TILES), :] for t in range(ROW_TILES)], axis=1)


def _outproj_body(x_ref, ya_ref, ys_ref, yr_ref, w_ref, g1_ref, sc_ref, sh_ref, ng_ref, wr_ref, br_ref,
                  xo_ref, h2_ref, route_ref):
    a = ATTN_WIDTH
    mixed = jnp.dot(ya_ref[...], w_ref[0:a, :], preferred_element_type=F32)
    mixed += jnp.dot(ys_ref[...], w_ref[a:a + SSM_WIDTH, :], preferred_element_type=F32)
    mixed += jnp.dot(yr_ref[...], w_ref[a + SSM_WIDTH:, :], preferred_element_type=F32)
    x = x_ref[...] + g1_ref[...] * mixed
    xo_ref[...] = x
    h2 = _modulated_norm(x, ng_ref[...], sc_ref[...], sh_ref[...])
    _store_row_tiled(h2_ref, h2)

    h_hi = h2.astype(BF16)
    h_lo = (h2 - h_hi.astype(F32)).astype(BF16)
    r_hi = jnp.dot(h_hi, wr_ref[...], preferred_element_type=F32)
    r_lo = jnp.dot(h_lo, wr_ref[...], preferred_element_type=F32)
    logits = (r_hi[:, :LANES] + ((r_hi[:, LANES:] + r_lo[:, :LANES]) + r_lo[:, LANES:])
              + br_ref[...])
    lane = lax.broadcasted_iota(jnp.int32, logits.shape, 1).astype(F32)
    big = float(LANES)

    def first_argmax(vals):
        m = jnp.max(vals, axis=-1, keepdims=True)
        return m, jnp.min(jnp.where(vals == m, lane, big), axis=-1, keepdims=True)

    gl = jnp.where(lane < N_GROUPS, logits, -jnp.inf)
    gmax, gidx = first_argmax(gl)
    group_w = 1.0 / jnp.sum(jnp.exp(gl - gmax), axis=-1, keepdims=True)
    e_lo = N_GROUPS + gidx * EXPERTS_PER_GROUP
    el = jnp.where((lane >= e_lo) & (lane < e_lo + EXPERTS_PER_GROUP), logits, -jnp.inf)
    v1, i1 = first_argmax(el)
    v2, i2 = first_argmax(jnp.where(lane == i1, -jnp.inf, el))
    t2 = jnp.exp(v2 - v1)
    w1 = group_w / (1.0 + t2)
    w2 = group_w * t2 / (1.0 + t2)
    route = jnp.where(lane == 0, i1 - N_GROUPS,
                      jnp.where(lane == 1, i2 - N_GROUPS,
                                jnp.where(lane == 2, w1, jnp.where(lane == 3, w2, 0.0))))
    route_ref[...] = route


def _outproj_router(x, ya, ys_tm, yr, w_out_bf16, g1, sc2, sh2, ng, w_route, b_route, ts):
    b, s, d = x.shape
    tok = lambda width: pl.BlockSpec((None, ts, width), lambda bi, j: (bi, j, 0))
    per_b = pl.BlockSpec((None, 1, d), lambda bi, j: (bi, 0, 0))
    full = lambda a: pl.BlockSpec(a.shape, lambda bi, j: (0,) * a.ndim)
    nj = s // ts
    return pl.pallas_call(
        _outproj_body,
        grid=(b, nj),
        in_specs=[tok(d), tok(ATTN_WIDTH),
                  pl.BlockSpec((ts, SSM_WIDTH), lambda bi, j: (j, bi)),
                  tok(RET_WIDTH), full(w_out_bf16), per_b, per_b, per_b, full(ng),
                  full(w_route), full(b_route)],
        out_specs=[tok(d),
                   pl.BlockSpec((ts * ROW_TILES, LANES), lambda bi, j: (bi * nj + j, 0)),
                   pl.BlockSpec((None, ts, LANES), lambda bi, j: (bi, j, 0))],
        out_shape=[jax.ShapeDtypeStruct((b, s, d), F32),
                   jax.ShapeDtypeStruct((b * s * ROW_TILES, LANES), F32),
                   jax.ShapeDtypeStruct((b, s, LANES), F32)],
        compiler_params=_cparams("parallel", "arbitrary"),
        name="outproj_router",
    )(x, ya, ys_tm, yr, w_out_bf16, g1, sc2, sh2, ng, w_route, b_route)


def _slot_plan(first_id, second_id, n_slots):
    tm = MOE_TM
    experts = jnp.arange(N_EXPERTS, dtype=jnp.int32)[:, None]
    first = (first_id[None, :] == experts).astype(jnp.int32)
    second = (second_id[None, :] == experts).astype(jnp.int32)
    both = first + second
    csum = jnp.cumsum(both, axis=1)
    counts = csum[:, -1]
    padded = ((counts + tm - 1) // tm) * tm
    ends = jnp.cumsum(padded)
    starts = ends - padded
    base = starts[:, None] + (csum - both)
    dest_first = jnp.sum(first * base, axis=0).astype(jnp.int32)
    dest_second = jnp.sum(second * (base + first), axis=0).astype(jnp.int32)
    n_tiles = n_slots // tm
    tile_row = jnp.arange(n_tiles, dtype=jnp.int32) * tm
    tile_expert = jnp.minimum(jnp.sum((tile_row[:, None] >= ends[None, :]).astype(jnp.int32), axis=1),
                              N_EXPERTS - 1).astype(jnp.int32)
    used_tiles = (ends[-1] // tm).astype(jnp.int32)
    group_last = jnp.where(padded > 0, ends - tm, n_slots)
    tail = jnp.where(tile_row >= ends[-1], tile_row, n_slots)
    fill = jnp.sort(jnp.concatenate([group_last, tail]).astype(jnp.int32))
    n_fill = jnp.sum(fill < n_slots).astype(jnp.int32)
    meta = jnp.stack([n_fill, used_tiles]).astype(jnp.int32)
    return (dest_first, dest_second), tile_expert, fill, meta


INPROJ_STAGES = 8
DMA_UNROLL = 8


def _dispatch_body(dest0_ref, dest1_ref, fill_ref, meta_ref, h2_ref, xs_hbm, zero_ref, sem):
    step = pl.program_id(0)
    tile_rows = MOE_TM * ROW_TILES
    tokens = h2_ref.shape[0] // ROW_TILES

    def fill_copy(i):
        row = pl.multiple_of(fill_ref[i] * ROW_TILES, tile_rows)
        return pltpu.make_async_copy(zero_ref, xs_hbm.at[pl.ds(row, tile_rows), :], sem)

    @pl.when(step == 0)
    def _():
        zero_ref[...] = jnp.zeros_like(zero_ref)
        n_fill = meta_ref[0]
        lax.fori_loop(0, n_fill, lambda i, c: (fill_copy(i).start(), c)[1], 0)
        lax.fori_loop(0, n_fill, lambda i, c: (fill_copy(i).wait(), c)[1], 0)

    base = step * tokens
    dest_refs = (dest0_ref, dest1_ref)

    def row_copy(i, choice):
        src = pl.multiple_of(i * ROW_TILES, ROW_TILES)
        dst = pl.multiple_of(dest_refs[choice][base + i] * ROW_TILES, ROW_TILES)
        return pltpu.make_async_copy(h2_ref.at[pl.ds(src, ROW_TILES), :],
                                     xs_hbm.at[pl.ds(dst, ROW_TILES), :], sem)

    def start(i, c):
        row_copy(i, 0).start(priority=0)
        row_copy(i, 1).start(priority=1)
        return c

    def wait(i, c):
        row_copy(i, 0).wait()
        row_copy(i, 1).wait()
        return c

    lax.fori_loop(0, tokens, start, 0, unroll=DMA_UNROLL)
    lax.fori_loop(0, tokens, wait, 0, unroll=DMA_UNROLL)


def _dispatch(h2_tiled, dest, fill, meta, n_slots):
    tokens = 2048
    rows = tokens * ROW_TILES
    return pl.pallas_call(
        _dispatch_body,
        grid_spec=pltpu.PrefetchScalarGridSpec(
            num_scalar_prefetch=4,
            grid=(h2_tiled.shape[0] // rows,),
            in_specs=[pl.BlockSpec((rows, LANES), lambda i, d0, d1, f, m: (i, 0))],
            out_specs=pl.BlockSpec(memory_space=pl.ANY),
            scratch_shapes=[pltpu.VMEM((MOE_TM * ROW_TILES, LANES), F32), pltpu.SemaphoreType.DMA],
        ),
        out_shape=jax.ShapeDtypeStruct((n_slots * ROW_TILES, LANES), F32),
        compiler_params=pltpu.CompilerParams(dimension_semantics=("arbitrary",), has_side_effects=True,
                                             vmem_limit_bytes=VMEM_LIMIT),
        name="moe_dispatch",
    )(*dest, fill, meta, h2_tiled)


def _experts_body(te_ref, meta_ref, xs_ref, wg_ref, wu_ref, wd_ref, ys_ref):
    tm = MOE_TM

    @pl.when(pl.program_id(0) < meta_ref[1])
    def _():
        x = _load_row_tiled(xs_ref, 0, tm).astype(BF16)
        hg = jnp.dot(x, wg_ref[...].astype(BF16), preferred_element_type=F32)
        hu = jnp.dot(x, wu_ref[...].astype(BF16), preferred_element_type=F32)
        act = (hg * _sigmoid(hg) * hu).astype(BF16)
        _store_row_tiled(ys_ref, jnp.dot(act, wd_ref[...].astype(BF16), preferred_element_type=F32))

    @pl.when(pl.program_id(0) >= meta_ref[1])
    def _():
        ys_ref[...] = jnp.zeros_like(ys_ref)


def _experts(xs, tile_expert, meta, w_gate, w_up, w_down, layer):
    n_tiles = tile_expert.shape[0]
    rows = MOE_TM * ROW_TILES
    d, f = w_gate.shape[2], w_gate.shape[3]
    return pl.pallas_call(
        _experts_body,
        grid_spec=pltpu.PrefetchScalarGridSpec(
            num_scalar_prefetch=2,
            grid=(n_tiles,),
            in_specs=[pl.BlockSpec((rows, LANES), lambda i, te, mt: (i, 0)),
                      pl.BlockSpec((None, None, d, f), lambda i, te, mt: (layer, te[i], 0, 0)),
                      pl.BlockSpec((None, None, d, f), lambda i, te, mt: (layer, te[i], 0, 0)),
                      pl.BlockSpec((None, None, f, d), lambda i, te, mt: (layer, te[i], 0, 0))],
            out_specs=pl.BlockSpec((rows, LANES), lambda i, te, mt: (i, 0)),
        ),
        out_shape=jax.ShapeDtypeStruct(xs.shape, F32),
        compiler_params=_cparams("arbitrary"),
        name="moe_experts",
    )(tile_expert, meta, xs, w_gate, w_up, w_down)


def _combine_body(dest0_ref, dest1_ref, x_ref, route_ref, g2_ref, fg_ref, ys_hbm, o_ref, buf_ref, sem, *,
                  final_norm):
    tm = x_ref.shape[0]
    base = (pl.program_id(0) * pl.num_programs(1) + pl.program_id(1)) * tm

    dest_refs = (dest0_ref, dest1_ref)

    def row_copy(i, choice):
        src = pl.multiple_of(dest_refs[choice][base + i] * ROW_TILES, ROW_TILES)
        dst = pl.multiple_of((choice * tm + i) * ROW_TILES, ROW_TILES)
        return pltpu.make_async_copy(ys_hbm.at[pl.ds(src, ROW_TILES), :],
                                     buf_ref.at[pl.ds(dst, ROW_TILES), :], sem)

    def start(i, c):
        row_copy(i, 0).start(priority=0)
        row_copy(i, 1).start(priority=1)
        return c

    def wait(i, c):
        row_copy(i, 0).wait()
        row_copy(i, 1).wait()
        return c

    lax.fori_loop(0, tm, start, 0, unroll=DMA_UNROLL)
    lax.fori_loop(0, tm, wait, 0, unroll=DMA_UNROLL)
    route = route_ref[...]
    y = route[:, 2:3] * _load_row_tiled(buf_ref, 0, tm) + route[:, 3:4] * _load_row_tiled(buf_ref, tm, tm)
    x = x_ref[...] + g2_ref[...] * y
    if final_norm:
        ms = jnp.mean(x * x, axis=-1, keepdims=True)
        x = x * lax.rsqrt(ms + NORM_EPS) * fg_ref[...]
    o_ref[...] = x


def _combine(x, route, g2, final_g, ys, dest, tm, final_norm):
    b, s, d = x.shape
    tok = lambda width: pl.BlockSpec((None, tm, width), lambda bi, j, d0, d1: (bi, j, 0))
    return pl.pallas_call(
        functools.partial(_combine_body, final_norm=final_norm),
        grid_spec=pltpu.PrefetchScalarGridSpec(
            num_scalar_prefetch=2,
            grid=(b, s // tm),
            in_specs=[tok(d), tok(LANES),
                      pl.BlockSpec((None, 1, d), lambda bi, j, d0, d1: (bi, 0, 0)),
                      pl.BlockSpec((1, d), lambda bi, j, d0, d1: (0, 0)),
                      pl.BlockSpec(memory_space=pl.ANY)],
            out_specs=tok(d),
            scratch_shapes=[pltpu.VMEM((2 * tm * ROW_TILES, LANES), F32), pltpu.SemaphoreType.DMA],
        ),
        out_shape=jax.ShapeDtypeStruct((b, s, d), F32),
        compiler_params=_cparams("arbitrary", "arbitrary"),
        name="moe_combine",
    )(*dest, x, route, g2, final_g, ys)


def _combine_inproj_body(dest0_ref, dest1_ref, x_ref, route_ref, g2_ref, ys_hbm, sc_ref, sh_ref, g_ref, w_ref, cos_ref,
                         sin_ref, xo_ref, qa_ref, ka_ref, va_ref, us_ref, qr_ref, kr_ref, vr_ref, gr_ref,
                         buf_ref, sems):
    tm = x_ref.shape[0]
    n_tiles = pl.num_programs(0) * pl.num_programs(1)
    tile = pl.program_id(0) * pl.num_programs(1) + pl.program_id(1)
    slot = tile % 2

    dest_refs = (dest0_ref, dest1_ref)

    def row_copy(t, s, i, choice):
        src = pl.multiple_of(dest_refs[choice][t * tm + i] * ROW_TILES, ROW_TILES)
        dst = pl.multiple_of(((2 * s + choice) * tm + i) * ROW_TILES, ROW_TILES)
        return pltpu.make_async_copy(ys_hbm.at[pl.ds(src, ROW_TILES), :],
                                     buf_ref.at[pl.ds(dst, ROW_TILES), :], sems.at[s])

    def start_rows(t, s, lo, hi):
        def start(i, c):
            row_copy(t, s, i, 0).start(priority=0)
            row_copy(t, s, i, 1).start(priority=1)
            return c
        lax.fori_loop(lo, hi, start, 0, unroll=DMA_UNROLL)

    @pl.when(tile == 0)
    def _():
        start_rows(tile, slot, 0, tm)

    chunk = tm // INPROJ_STAGES
    nxt = jnp.minimum(tile + 1, n_tiles - 1)

    def prefetch_chunk(k):
        for i in range(k * chunk, (k + 1) * chunk):
            row_copy(nxt, 1 - slot, i, 0).start(priority=0)
            row_copy(nxt, 1 - slot, i, 1).start(priority=1)

    def wait_tile(t, s):
        def wait(i, c):
            row_copy(t, s, i, 0).wait()
            row_copy(t, s, i, 1).wait()
            return c
        lax.fori_loop(0, tm, wait, 0, unroll=DMA_UNROLL)

    wait_tile(tile, slot)
    route = route_ref[...]
    first = 2 * slot * tm
    y = (route[:, 2:3] * _load_row_tiled(buf_ref, first, tm)
         + route[:, 3:4] * _load_row_tiled(buf_ref, first + tm, tm))
    x = x_ref[...] + g2_ref[...] * y
    xo_ref[...] = x
    _inproj_project(x, sc_ref, sh_ref, g_ref, w_ref, cos_ref, sin_ref,
                    qa_ref, ka_ref, va_ref, us_ref, qr_ref, kr_ref, vr_ref, gr_ref, between=prefetch_chunk)

    @pl.when(tile == n_tiles - 1)
    def _():
        wait_tile(nxt, 1 - slot)


def _combine_inproj(x, route, g2, ys, dest, sc, sh, g, w_bf16, cos_t, sin_t, ts):
    b, s, d = x.shape
    tok = lambda width: pl.BlockSpec((None, ts, width), lambda bi, j, d0, d1: (bi, j, 0))
    per_b = pl.BlockSpec((None, 1, d), lambda bi, j, d0, d1: (bi, 0, 0))
    rope = pl.BlockSpec((ts, RET_WIDTH), lambda bi, j, d0, d1: (j, 0))
    sds = lambda width, dt: jax.ShapeDtypeStruct((b, s, width), dt)
    return pl.pallas_call(
        _combine_inproj_body,
        grid_spec=pltpu.PrefetchScalarGridSpec(
            num_scalar_prefetch=2,
            grid=(b, s // ts),
            in_specs=[tok(d), tok(LANES), per_b, pl.BlockSpec(memory_space=pl.ANY), per_b, per_b,
                      pl.BlockSpec((1, d), lambda bi, j, d0, d1: (0, 0)),
                      pl.BlockSpec(w_bf16.shape, lambda bi, j, d0, d1: (0, 0)),
                      rope, rope],
            out_specs=[tok(d), tok(ATTN_WIDTH), tok(ATTN_WIDTH), tok(ATTN_WIDTH),
                       pl.BlockSpec((ts, SSM_WIDTH), lambda bi, j, d0, d1: (j, bi)),
                       tok(RET_WIDTH), tok(RET_WIDTH), tok(RET_WIDTH), tok(RET_WIDTH)],
            scratch_shapes=[pltpu.VMEM((2 * 2 * ts * ROW_TILES, LANES), F32), pltpu.SemaphoreType.DMA((2,))],
        ),
        out_shape=[sds(d, F32), sds(ATTN_WIDTH, BF16), sds(ATTN_WIDTH, BF16), sds(ATTN_WIDTH, BF16),
                   jax.ShapeDtypeStruct((s, b * SSM_WIDTH), F32),
                   sds(RET_WIDTH, BF16), sds(RET_WIDTH, BF16), sds(RET_WIDTH, BF16), sds(RET_WIDTH, F32)],
        compiler_params=_cparams("arbitrary", "arbitrary"),
        name="combine_inproj",
    )(*dest, x, route, g2, ys, sc, sh, g, w_bf16, cos_t, sin_t)


def _rope_tables(s):
    half = HEAD_DIM // 2
    inv_freq = ROPE_BASE ** (-jnp.arange(half, dtype=F32) / half)
    ang = jnp.arange(s, dtype=F32)[:, None] * inv_freq[None, :]
    cos = jnp.tile(jnp.cos(ang), (1, 2 * RET_HEADS))
    sin = jnp.sin(ang)
    sin = jnp.tile(jnp.concatenate([-sin, sin], axis=1), (1, RET_HEADS))
    return cos, sin


def _trunk(x, c, norm1_g, norm2_g, w_ada, b_ada, w_in, attn_rel_bias, ssm_a_re, ssm_a_im, ssm_log_dt,
           ssm_b_re, ssm_b_im, ssm_c_re, ssm_c_im, ssm_d, ssm_w_glu, ssm_b_glu, ret_gn_g, w_out,
           moe_w_group, moe_b_group, moe_w_expert, moe_b_expert, moe_w_gate, moe_w_up, moe_w_down, final_g,
           *, row_tile):
    b, s, d = x.shape
    n_layers = w_in.shape[0]
    n_slots = 2 * b * s + N_EXPERTS * MOE_TM
    mod = _adaln(c, w_ada, b_ada).reshape(n_layers, b, 6, 1, d)
    cos_t, sin_t = _rope_tables(s)
    ret_tables = _ret_tables(RET_T)
    w_in_b = w_in.astype(BF16)
    w_out_b = w_out.astype(BF16)
    n_route = N_GROUPS + N_EXPERTS
    w_route = jnp.pad(jnp.concatenate([moe_w_group, moe_w_expert], axis=-1).astype(F32),
                      ((0, 0), (0, 0), (0, LANES - n_route)))
    w_route_hi = w_route.astype(BF16)
    w_route = jnp.concatenate([w_route_hi, (w_route - w_route_hi.astype(F32)).astype(BF16)], axis=-1)
    b_route = jnp.pad(jnp.concatenate([moe_b_group, moe_b_expert], axis=-1).astype(F32),
                      ((0, 0), (0, LANES - n_route))).reshape(n_layers, 1, LANES)
    fg = final_g.reshape(1, d).astype(F32)
    projected = _inproj(x, mod[0, :, 1], mod[0, :, 0], norm1_g[0].reshape(1, d), w_in_b[0], cos_t, sin_t, row_tile)
    for i in range(n_layers):
        _, _, g1, sh2, sc2, g2 = (mod[i, :, k] for k in range(6))
        qa, ka, va, us, qr, kr, vr, gr = projected
        y_a = _attention(qa, ka, va, _attn_bias_table(attn_rel_bias[i], ATTN_TQ))
        bm, cm, ar, ai = _ssm_params(ssm_a_re[i], ssm_a_im[i], ssm_log_dt[i], ssm_b_re[i], ssm_b_im[i],
                                     ssm_c_re[i], ssm_c_im[i])
        y_s = _ssm(us, b, bm, cm, ar, ai, ssm_d[i], ssm_w_glu[i], ssm_b_glu[i])
        y_r = _retention(qr, kr, vr, gr, ret_gn_g[i], ret_tables)
        x, h2, route = _outproj_router(x, y_a, y_s, y_r, w_out_b[i], g1, sc2, sh2,
                                       norm2_g[i].reshape(1, d), w_route[i], b_route[i], row_tile)
        first_id = route[:, :, 0].astype(jnp.int32).reshape(b * s)
        second_id = route[:, :, 1].astype(jnp.int32).reshape(b * s)
        dest, tile_expert, fill, meta = _slot_plan(first_id, second_id, n_slots)
        xs = _dispatch(h2, dest, fill, meta, n_slots)
        ys = _experts(xs, tile_expert, meta, moe_w_gate, moe_w_up, moe_w_down, i)
        if i == n_layers - 1:
            x = _combine(x, route, g2, fg, ys, dest, row_tile, final_norm=True)
        else:
            x, *projected = _combine_inproj(x, route, g2, ys, dest, mod[i + 1, :, 1], mod[i + 1, :, 0],
                                            norm1_g[i + 1].reshape(1, d), w_in_b[i + 1], cos_t, sin_t, row_tile)
    return x


def kernel(x, c, norm1_g, norm2_g, w_ada, b_ada, w_in, attn_rel_bias, ssm_a_re, ssm_a_im, ssm_log_dt, ssm_b_re, ssm_b_im, ssm_c_re, ssm_c_im, ssm_d, ssm_w_glu, ssm_b_glu, ret_gn_g, w_out, moe_w_group, moe_b_group, moe_w_expert, moe_b_expert, moe_w_gate, moe_w_up, moe_w_down, final_g):
    return _trunk(x, c, norm1_g, norm2_g, w_ada, b_ada, w_in, attn_rel_bias, ssm_a_re, ssm_a_im, ssm_log_dt,
                  ssm_b_re, ssm_b_im, ssm_c_re, ssm_c_im, ssm_d, ssm_w_glu, ssm_b_glu, ret_gn_g, w_out,
                  moe_w_group, moe_b_group, moe_w_expert, moe_b_expert, moe_w_gate, moe_w_up, moe_w_down,
                  final_g, row_tile=512)
```
